```python
import math
import jax, jax.numpy as jnp
from jax import lax
import numpy as np

D_MODEL = 2048
BATCH = 4
SEQ = 2048
DEPTH = 2

GRID_W = 64
CTX_LEN = 256
N_MIXERS = 2
N_A_LAYERS = (DEPTH + 1) // 2
N_B_LAYERS = DEPTH // 2

S5_GROUP = 16
S5_GROUPS = D_MODEL // S5_GROUP
S5_STATE = 64
S5_DT_MIN = 1e-3
S5_DT_MAX = 1e-1

HEAD_DIM = 128
N_Q_HEADS = D_MODEL // HEAD_DIM
N_KV_HEADS = N_Q_HEADS // 4
KV_REP = N_Q_HEADS // N_KV_HEADS
Q_DIM = N_Q_HEADS * HEAD_DIM
KV_DIM = N_KV_HEADS * HEAD_DIM
Q_BLOCK = 128
ROPE_THETA = 10000.0
ROPE_PAIRS_PER_AXIS = HEAD_DIM // 4

D_FF = 5632
N_MOD = 9
EPS = 1e-6

kernel_name = "hybrid_s5_gqa_macaron_dit"


def rmsnorm(x, g):
    xf = x.astype(jnp.float32)
    r = lax.rsqrt(jnp.mean(xf * xf, axis=-1, keepdims=True) + EPS)
    return (xf * r * g.astype(jnp.float32)).astype(x.dtype)


def modulate(h, shift, scale):
    return h * (1.0 + scale) + shift


def swiglu(h, w_gate, w_up, w_down):
    return (jax.nn.silu(h @ w_gate) * (h @ w_up)) @ w_down


def s5_scan(u, a_re, a_im, log_dt, b_re, b_im, c_re, c_im, reverse):
    lam = lax.complex(a_re.astype(jnp.float32), a_im.astype(jnp.float32))
    dt = jnp.exp(log_dt.astype(jnp.float32))[:, None]
    lam_bar = jnp.exp(lam * dt)
    b = lax.complex(b_re.astype(jnp.float32), b_im.astype(jnp.float32))
    b_bar = ((lam_bar - 1.0) / lam)[..., None] * b
    bu = jnp.einsum('gph,btgh->btgp', b_bar, u.astype(jnp.complex64))
    a = jnp.broadcast_to(lam_bar, bu.shape)

    def combine(left, right):
        a_l, b_l = left
        a_r, b_r = right
        return a_r * a_l, a_r * b_l + b_r

    _, h = lax.associative_scan(combine, (a, bu), axis=1, reverse=reverse)
    cm = lax.complex(c_re.astype(jnp.float32), c_im.astype(jnp.float32))
    return jnp.einsum('ghp,btgp->btgh', cm, h).real


def s5_output(y, u, d_skip, w1, w2):
    y = y + d_skip.astype(jnp.float32) * u.astype(jnp.float32)
    z = jax.nn.gelu(y).astype(u.dtype)
    return (z @ w1) * jax.nn.sigmoid(z @ w2)


def s5_mixer(h_ctx, h_lat, a_re, a_im, log_dt, b_re, b_im, c_re, c_im, d_skip, w1, w2, need_ctx):
    B, S, D = h_lat.shape
    L = h_ctx.shape[1]
    u_ctx = h_ctx.astype(jnp.float32).reshape(B, L, S5_GROUPS, S5_GROUP)
    u_lat = h_lat.astype(jnp.float32).reshape(B, S, S5_GROUPS, S5_GROUP)
    y_f = s5_scan(jnp.concatenate([u_ctx, u_lat], axis=1), a_re[0], a_im[0], log_dt[0],
                  b_re[0], b_im[0], c_re[0], c_im[0], reverse=False)
    y_b = s5_scan(jnp.concatenate([u_lat, u_ctx], axis=1), a_re[1], a_im[1], log_dt[1],
                  b_re[1], b_im[1], c_re[1], c_im[1], reverse=True)
    y_lat = (y_f[:, L:] + y_b[:, :S]).reshape(B, S, D)
    out_lat = s5_output(y_lat, h_lat, d_skip, w1, w2)
    out_ctx = None
    if need_ctx:
        y_ctx = (y_f[:, :L] + y_b[:, S:]).reshape(B, L, D)
        out_ctx = s5_output(y_ctx, h_ctx, d_skip, w1, w2)
    return out_ctx, out_lat


def axial_rope_tables(pos_row, pos_col):
    freqs = ROPE_THETA ** (-jnp.arange(ROPE_PAIRS_PER_AXIS, dtype=jnp.float32) / ROPE_PAIRS_PER_AXIS)
    ang_r = pos_row.astype(jnp.float32)[:, None] * freqs
    ang_c = pos_col.astype(jnp.float32)[:, None] * freqs
    return jnp.cos(ang_r), jnp.sin(ang_r), jnp.cos(ang_c), jnp.sin(ang_c)


def rotate(x, cos, sin):
    x1, x2 = jnp.split(x, 2, axis=-1)
    c = cos[:, None, :].astype(x.dtype)
    s = sin[:, None, :].astype(x.dtype)
    return jnp.concatenate([x1 * c - x2 * s, x2 * c + x1 * s], axis=-1)


def apply_axial_rope(x, tabs):
    cos_r, sin_r, cos_c, sin_c = tabs
    half = HEAD_DIM // 2
    return jnp.concatenate([rotate(x[..., :half], cos_r, sin_r),
                            rotate(x[..., half:], cos_c, sin_c)], axis=-1)


def attend(q, k, v):
    s = jnp.einsum('bqgrd,bkgd->bgrqk', q, k).astype(jnp.float32) * (HEAD_DIM ** -0.5)
    p = jax.nn.softmax(s, axis=-1)
    return jnp.einsum('bgrqk,bkgd->bqgrd', p.astype(v.dtype), v)


def gqa_mixer(h_ctx, h_lat, w_qkv, w_o, q_gain, k_gain, rope_tabs, need_ctx):
    B, S, D = h_lat.shape
    L = h_ctx.shape[1]

    def project(h):
        T = h.shape[1]
        qkv = h @ w_qkv
        q = rmsnorm(qkv[..., :Q_DIM].reshape(B, T, N_Q_HEADS, HEAD_DIM), q_gain)
        k = rmsnorm(qkv[..., Q_DIM:Q_DIM + KV_DIM].reshape(B, T, N_KV_HEADS, HEAD_DIM), k_gain)
        v = qkv[..., Q_DIM + KV_DIM:].reshape(B, T, N_KV_HEADS, HEAD_DIM)
        return q, k, v

    q_c, k_c, v_c = project(h_ctx)
    q_l, k_l, v_l = project(h_lat)
    q_l = apply_axial_rope(q_l, rope_tabs)
    k_l = apply_axial_rope(k_l, rope_tabs)
    k_all = jnp.concatenate([k_c, k_l], axis=1)
    v_all = jnp.concatenate([v_c, v_l], axis=1)

    n_blk = S // Q_BLOCK
    q_blocks = q_l.reshape(B, n_blk, Q_BLOCK, N_KV_HEADS, KV_REP, HEAD_DIM).transpose(1, 0, 2, 3, 4, 5)
    o_l = lax.map(lambda qb: attend(qb, k_all, v_all), q_blocks)
    o_l = o_l.transpose(1, 0, 2, 3, 4, 5).reshape(B, S, Q_DIM)
    out_lat = o_l @ w_o
    out_ctx = None
    if need_ctx:
        o_c = attend(q_c.reshape(B, L, N_KV_HEADS, KV_REP, HEAD_DIM), k_c, v_c).reshape(B, L, Q_DIM)
        out_ctx = o_c @ w_o
    return out_ctx, out_lat


def setup_inputs(seed: int = 0) -> dict:
    key = jax.random.key(seed)
    ks = jax.random.split(key, 32)
    f32 = jnp.float32
    D, F, G, H, P = D_MODEL, D_FF, S5_GROUPS, S5_GROUP, S5_STATE

    def nrm(k, shape, scale):
        return jax.random.normal(k, shape, f32) * scale

    a_im_base = jnp.pi * jnp.arange(P, dtype=f32)
    return {
        "x": nrm(ks[0], (BATCH, SEQ, D), 1.0),
        "c": nrm(ks[1], (BATCH, D), 1.0),
        "ctx": nrm(ks[2], (BATCH, CTX_LEN, D), 1.0),
        "c_ctx": nrm(ks[3], (D,), 1.0),
        "ada_w": nrm(ks[4], (DEPTH, D, N_MOD * D), 0.5 * D ** -0.5),
        "ada_b": nrm(ks[5], (DEPTH, N_MOD * D), 0.02),
        "norm_ffn1": 1.0 + nrm(ks[6], (DEPTH, D), 0.02),
        "norm_mix": 1.0 + nrm(ks[7], (DEPTH, D), 0.02),
        "norm_ffn2": 1.0 + nrm(ks[8], (DEPTH, D), 0.02),
        "ffn1_w_gate": nrm(ks[9], (DEPTH, D, F), D ** -0.5),
        "ffn1_w_up": nrm(ks[10], (DEPTH, D, F), D ** -0.5),
        "ffn1_w_down": nrm(ks[11], (DEPTH, F, D), F ** -0.5),
        "ffn2_w_gate": nrm(ks[12], (DEPTH, D, F), D ** -0.5),
        "ffn2_w_up": nrm(ks[13], (DEPTH, D, F), D ** -0.5),
        "ffn2_w_down": nrm(ks[14], (DEPTH, F, D), F ** -0.5),
        "s5_a_re": -0.5 * jnp.exp(nrm(ks[15], (N_A_LAYERS, 2, G, P), 0.1)),
        "s5_a_im": a_im_base + nrm(ks[16], (N_A_LAYERS, 2, G, P), 0.01),
        "s5_log_dt": jax.random.uniform(ks[17], (N_A_LAYERS, 2, G), f32,
                                        math.log(S5_DT_MIN), math.log(S5_DT_MAX)),
        "s5_b_re": nrm(ks[18], (N_A_LAYERS, 2, G, P, H), (2 * H) ** -0.5),
        "s5_b_im": nrm(ks[19], (N_A_LAYERS, 2, G, P, H), (2 * H) ** -0.5),
        "s5_c_re": nrm(ks[20], (N_A_LAYERS, 2, G, H, P), P ** -0.5),
        "s5_c_im": nrm(ks[21], (N_A_LAYERS, 2, G, H, P), P ** -0.5),
        "s5_d": nrm(ks[22], (N_A_LAYERS, D), 1.0),
        "s5_glu_w1": nrm(ks[23], (N_A_LAYERS, D, D), D ** -0.5),
        "s5_glu_w2": nrm(ks[24], (N_A_LAYERS, D, D), D ** -0.5),
        "attn_w_qkv": nrm(ks[25], (N_B_LAYERS, D, Q_DIM + 2 * KV_DIM), D ** -0.5),
        "attn_w_o": nrm(ks[26], (N_B_LAYERS, Q_DIM, D), Q_DIM ** -0.5),
        "attn_q_gain": 1.0 + nrm(ks[27], (N_B_LAYERS, HEAD_DIM), 0.02),
        "attn_k_gain": 1.0 + nrm(ks[28], (N_B_LAYERS, HEAD_DIM), 0.02),
    }


def reference(x, c, ctx, c_ctx, ada_w, ada_b, norm_ffn1, norm_mix, norm_ffn2,
              ffn1_w_gate, ffn1_w_up, ffn1_w_down, ffn2_w_gate, ffn2_w_up, ffn2_w_down,
              s5_a_re, s5_a_im, s5_log_dt, s5_b_re, s5_b_im, s5_c_re, s5_c_im, s5_d,
              s5_glu_w1, s5_glu_w2, attn_w_qkv, attn_w_o, attn_q_gain, attn_k_gain):
    B, S, D = x.shape
    ROWS = S // GRID_W
    pos_row = jnp.repeat(jnp.arange(ROWS, dtype=jnp.int32), GRID_W)
    pos_col = jnp.tile(jnp.arange(GRID_W, dtype=jnp.int32), ROWS)
    rope_tabs = axial_rope_tables(pos_row, pos_col)

    silu_c = jax.nn.silu(c)
    silu_cc = jax.nn.silu(c_ctx)
    h_lat, h_ctx = x, ctx

    for i in range(DEPTH):
        last = i == DEPTH - 1
        mod_l = [m[:, None, :] for m in jnp.split(silu_c @ ada_w[i] + ada_b[i], N_MOD, axis=-1)]
        mod_c = [m[None, None, :] for m in jnp.split(silu_cc @ ada_w[i] + ada_b[i], N_MOD, axis=-1)]
        sh1, sc1, g1, shm, scm, gm, sh2, sc2, g2 = mod_l
        csh1, csc1, cg1, cshm, cscm, cgm, csh2, csc2, cg2 = mod_c

        w1 = (ffn1_w_gate[i], ffn1_w_up[i], ffn1_w_down[i])
        h_lat = h_lat + 0.5 * g1 * swiglu(modulate(rmsnorm(h_lat, norm_ffn1[i]), sh1, sc1), *w1)
        h_ctx = h_ctx + 0.5 * cg1 * swiglu(modulate(rmsnorm(h_ctx, norm_ffn1[i]), csh1, csc1), *w1)

        u_lat = modulate(rmsnorm(h_lat, norm_mix[i]), shm, scm)
        u_ctx = modulate(rmsnorm(h_ctx, norm_mix[i]), cshm, cscm)
        if i % N_MIXERS == 0:
            j = i // N_MIXERS
            o_ctx, o_lat = s5_mixer(u_ctx, u_lat, s5_a_re[j], s5_a_im[j], s5_log_dt[j],
                                    s5_b_re[j], s5_b_im[j], s5_c_re[j], s5_c_im[j], s5_d[j],
                                    s5_glu_w1[j], s5_glu_w2[j], need_ctx=not last)
        else:
            j = i // N_MIXERS
            o_ctx, o_lat = gqa_mixer(u_ctx, u_lat, attn_w_qkv[j], attn_w_o[j], attn_q_gain[j],
                                     attn_k_gain[j], rope_tabs, need_ctx=not last)
        h_lat = h_lat + gm * o_lat
        if not last:
            h_ctx = h_ctx + cgm * o_ctx

        w2 = (ffn2_w_gate[i], ffn2_w_up[i], ffn2_w_down[i])
        h_lat = h_lat + 0.5 * g2 * swiglu(modulate(rmsnorm(h_lat, norm_ffn2[i]), sh2, sc2), *w2)
        if not last:
            h_ctx = h_ctx + 0.5 * cg2 * swiglu(modulate(rmsnorm(h_ctx, norm_ffn2[i]), csh2, csc2), *w2)

    return h_lat
```

```python
import functools
import math

import jax
import jax.numpy as jnp
from jax import lax
from jax.experimental import pallas as pl
from jax.experimental.pallas import tpu as pltpu

F32 = jnp.float32
BF16 = jnp.bfloat16

EPS = 1e-6
N_MOD = 9
GRID_W = 64
HEAD_DIM = 128
KV_REP = 4
ROPE_THETA = 10000.0
S5_GROUP = 16
S5_STATE = 64

V7X_LANES = 128
V7X_SUBLANES = 8
V7X_VMEM_LIMIT = 60 * 1024 * 1024

S5_GB = 8
S5_TQ = 64
S5_PITCH = 68


def _cparams(sem):
    return pltpu.CompilerParams(dimension_semantics=sem, vmem_limit_bytes=V7X_VMEM_LIMIT)


def _ada_kernel(a_ref, w_ref, b_ref, o_ref):
    a = a_ref[...]
    s = (a * jax.nn.sigmoid(a)).astype(BF16)
    o_ref[...] = jnp.dot(s, w_ref[...].astype(BF16), preferred_element_type=F32) + b_ref[...]


def _ada_mods(cond, ada_w, ada_b, tn=1024):
    depth, d, n = ada_w.shape
    rows = cond.shape[0]
    return pl.pallas_call(
        _ada_kernel,
        grid=(depth, n // tn),
        in_specs=[
            pl.BlockSpec((rows, d), lambda l, j: (0, 0)),
            pl.BlockSpec((None, d, tn), lambda l, j: (l, 0, j)),
            pl.BlockSpec((None, 1, tn), lambda l, j: (l, 0, j)),
        ],
        out_specs=pl.BlockSpec((None, rows, tn), lambda l, j: (l, 0, j)),
        out_shape=jax.ShapeDtypeStruct((depth, rows, n), F32),
        compiler_params=_cparams(("arbitrary", "arbitrary")),
        name="ada_mods",
    )(cond, ada_w, ada_b.reshape(depth, 1, n))


def _mod_spec(layer, col, d, bidx):
    return pl.BlockSpec((None, None, 1, d), lambda i, j: (layer, bidx(i), 0, col))


def _norm_mod(x, g, shift, scale):
    r = lax.rsqrt(jnp.mean(x * x, axis=-1, keepdims=True) + EPS)
    return (x * r * g) * (1.0 + scale) + shift


def _ffn_kernel(h_ref, nrm_ref, sh_ref, sc_ref, gt_ref, wg_ref, wu_ref, wd_ref, o_ref, m_ref, *, n_out_blk):
    f = pl.program_id(1)

    @pl.when(f == 0)
    def _():
        x = h_ref[...]
        m_ref[...] = _norm_mod(x, nrm_ref[...], sh_ref[...], sc_ref[...]).astype(BF16)
        o_ref[...] = x

    m = m_ref[...]
    g = jnp.dot(m, wg_ref[...].astype(BF16), preferred_element_type=F32)
    u = jnp.dot(m, wu_ref[...].astype(BF16), preferred_element_type=F32)
    a = (g * jax.nn.sigmoid(g) * u).astype(BF16)
    d = o_ref.shape[1]
    blk = d // n_out_blk
    for nb in range(n_out_blk):
        cols = slice(nb * blk, (nb + 1) * blk)
        p = jnp.dot(a, wd_ref[:, cols].astype(BF16), preferred_element_type=F32)
        o_ref[:, cols] += (0.5 * gt_ref[:, cols]) * p


def _ffn(h, mods, layer, mod_base, norm_g, w_gate, w_up, w_down, *, tm, bidx, tf=256):
    n_tok, d = h.shape
    f_dim = w_gate.shape[2]
    return pl.pallas_call(
        functools.partial(_ffn_kernel, n_out_blk=4),
        grid=(n_tok // tm, f_dim // tf),
        in_specs=[
            pl.BlockSpec((tm, d), lambda i, f: (i, 0), pipeline_mode=pl.Buffered(1)),
            pl.BlockSpec((None, 1, d), lambda i, f: (layer, 0, 0)),
            _mod_spec(layer, mod_base + 0, d, bidx),
            _mod_spec(layer, mod_base + 1, d, bidx),
            _mod_spec(layer, mod_base + 2, d, bidx),
            pl.BlockSpec((None, d, tf), lambda i, f: (layer, 0, f)),
            pl.BlockSpec((None, d, tf), lambda i, f: (layer, 0, f)),
            pl.BlockSpec((None, tf, d), lambda i, f: (layer, f, 0)),
        ],
        out_specs=pl.BlockSpec((tm, d), lambda i, f: (i, 0)),
        out_shape=jax.ShapeDtypeStruct((n_tok, d), F32),
        scratch_shapes=[pltpu.VMEM((tm, d), BF16)],
        compiler_params=_cparams(("arbitrary", "arbitrary")),
        name="ffn",
    )(h, norm_g.reshape(norm_g.shape[0], 1, d), mods, mods, mods, w_gate, w_up, w_down)


def _normmod_kernel(h_ref, nrm_ref, sh_ref, sc_ref, o_ref):
    o_ref[...] = _norm_mod(h_ref[...], nrm_ref[...], sh_ref[...], sc_ref[...])


def _normmod(h, mods, layer, mod_base, norm_g, *, tm, bidx):
    n_tok, d = h.shape
    return pl.pallas_call(
        _normmod_kernel,
        grid=(n_tok // tm, 1),
        in_specs=[
            pl.BlockSpec((tm, d), lambda i, j: (i, 0)),
            pl.BlockSpec((None, 1, d), lambda i, j: (layer, 0, 0)),
            _mod_spec(layer, mod_base + 0, d, bidx),
            _mod_spec(layer, mod_base + 1, d, bidx),
        ],
        out_specs=pl.BlockSpec((tm, d), lambda i, j: (i, 0)),
        out_shape=jax.ShapeDtypeStruct((n_tok, d), F32),
        compiler_params=_cparams(("arbitrary", "arbitrary")),
        name="normmod",
    )(h, norm_g.reshape(norm_g.shape[0], 1, d), mods, mods)


def _s5_discretize(a_re, a_im, log_dt, b_re, b_im):
    dt = jnp.exp(log_dt)[..., None]
    mag = jnp.exp(a_re * dt)
    lr = mag * jnp.cos(a_im * dt)
    li = mag * jnp.sin(a_im * dt)
    den = a_re * a_re + a_im * a_im
    cr = ((lr - 1.0) * a_re + li * a_im) / den
    ci = (li * a_re - (lr - 1.0) * a_im) / den
    bbr = cr[..., None] * b_re - ci[..., None] * b_im
    bbi = cr[..., None] * b_im + ci[..., None] * b_re
    return lr, li, bbr, bbi


def _block_diag(w, n_gb):
    dirs, g, r, c = w.shape
    gpb = g // n_gb
    w = w.reshape(dirs, n_gb, gpb, r, c)
    eye = jnp.eye(gpb, dtype=w.dtype)
    out = w[:, :, :, :, None, :] * eye[None, None, :, None, :, None]
    return out.reshape(dirs, n_gb, gpb * r, gpb * c)


def _pair_rows(lam, n_pairs):
    dirs = lam.shape[0]
    flat = lam.reshape(dirs, 2, n_pairs, 1, V7X_LANES)
    flat = jnp.broadcast_to(flat, (dirs, 2, n_pairs, 4, V7X_LANES))
    return jnp.transpose(flat, (0, 2, 1, 3, 4)).reshape(dirs, n_pairs, V7X_SUBLANES, V7X_LANES)


def _s5_scan_kernel(ul_ref, uc_ref, bre_ref, bim_ref, cre_ref, cim_ref, are_ref, aim_ref,
                    yl_ref, yc_ref, u_st, s_re, s_im, h_re, h_im, *, n_ctx_chunks, tq, pitch):
    d = pl.program_id(0)
    c = pl.program_id(1)
    n_b = ul_ref.shape[0]
    half_rows = n_b * pitch
    n_pairs = s_re.shape[0]
    tiles_per_gb = 2 * n_pairs // S5_GB
    gb_per_half = S5_GB // 2
    lanes_in = u_st.shape[1] // S5_GB

    @pl.when(c == 0)
    def _():
        h_re[...] = jnp.zeros_like(h_re)
        h_im[...] = jnp.zeros_like(h_im)
        u_st[...] = jnp.zeros_like(u_st)

    is_ctx = c < n_ctx_chunks

    @pl.when(is_ctx)
    def _():
        for b in range(n_b):
            u_st[b * pitch:b * pitch + tq, :] = uc_ref[b]

    @pl.when(jnp.logical_not(is_ctx))
    def _():
        for b in range(n_b):
            u_st[b * pitch:b * pitch + tq, :] = ul_ref[b]

    for gb in range(S5_GB):
        lhs = u_st[:, gb * lanes_in:(gb + 1) * lanes_in].astype(BF16)
        p_re = jnp.dot(lhs, bre_ref[gb], preferred_element_type=F32)
        p_im = jnp.dot(lhs, bim_ref[gb], preferred_element_type=F32)
        half = gb // gb_per_half
        rows = slice(half * half_rows, (half + 1) * half_rows)
        for j in range(tiles_per_gb):
            n = (gb % gb_per_half) * tiles_per_gb + j
            s_re[n, rows, :] = p_re[:, j * V7X_LANES:(j + 1) * V7X_LANES]
            s_im[n, rows, :] = p_im[:, j * V7X_LANES:(j + 1) * V7X_LANES]

    per_pass = 8
    for k in range(n_pairs // per_pass):
        ns = list(range(k * per_pass, (k + 1) * per_pass))
        a_r = [are_ref[n] for n in ns]
        a_i = [aim_ref[n] for n in ns]
        init = (tuple(h_re[n] for n in ns), tuple(h_im[n] for n in ns))

        def body(t, carry, ns=ns, a_r=a_r, a_i=a_i):
            tt = jnp.where(d == 0, t, tq - 1 - t)
            hr, hi = carry
            out_r, out_i = [], []
            for idx, n in enumerate(ns):
                rows = pl.ds(tt, V7X_SUBLANES, stride=pitch)
                x_r = s_re[n, rows, :]
                x_i = s_im[n, rows, :]
                n_r = a_r[idx] * hr[idx] - a_i[idx] * hi[idx] + x_r
                n_i = a_r[idx] * hi[idx] + a_i[idx] * hr[idx] + x_i
                s_re[n, rows, :] = n_r
                s_im[n, rows, :] = n_i
                out_r.append(n_r)
                out_i.append(n_i)
            return tuple(out_r), tuple(out_i)

        fin_r, fin_i = lax.fori_loop(0, tq, body, init)
        for idx, n in enumerate(ns):
            h_re[n] = fin_r[idx]
            h_im[n] = fin_i[idx]

    for gb in range(S5_GB):
        half = gb // gb_per_half
        rows = slice(half * half_rows, (half + 1) * half_rows)
        base = (gb % gb_per_half) * tiles_per_gb
        l_re = jnp.concatenate([s_re[base + j, rows, :] for j in range(tiles_per_gb)], axis=1).astype(BF16)
        l_im = jnp.concatenate([s_im[base + j, rows, :] for j in range(tiles_per_gb)], axis=1).astype(BF16)
        y = (jnp.dot(l_re, cre_ref[gb], preferred_element_type=F32)
             + jnp.dot(l_im, cim_ref[gb], preferred_element_type=F32))
        cols = slice(gb * lanes_in, (gb + 1) * lanes_in)

        @pl.when(is_ctx)
        def _(y=y, cols=cols):
            for b in range(n_b):
                yc_ref[b, :, cols] = y[b * pitch:b * pitch + tq]

        @pl.when(jnp.logical_not(is_ctx))
        def _(y=y, cols=cols):
            for b in range(n_b):
                yl_ref[b, :, cols] = y[b * pitch:b * pitch + tq]


def _s5_scan(u_lat, u_ctx, bd_bre, bd_bim, bd_cre, bd_cim, a_re, a_im):
    n_b, s_len, d = u_lat.shape
    l_len = u_ctx.shape[1]
    tq, pitch = S5_TQ, S5_PITCH
    n_l, n_s = l_len // tq, s_len // tq
    n_pairs = a_re.shape[1]

    def lat_idx(dd, c):
        k = jnp.maximum(c - n_l, 0)
        return jnp.where(dd == 0, k, n_s - 1 - k)

    def ctx_idx(dd, c):
        k = jnp.minimum(c, n_l - 1)
        return jnp.where(dd == 0, k, n_l - 1 - k)

    def wspec(shape):
        return pl.BlockSpec((None,) + shape, lambda dd, c: (dd,) + (0,) * len(shape),
                            pipeline_mode=pl.Buffered(1))

    kern = functools.partial(_s5_scan_kernel, n_ctx_chunks=n_l, tq=tq, pitch=pitch)
    return pl.pallas_call(
        kern,
        grid=(2, n_l + n_s),
        in_specs=[
            pl.BlockSpec((n_b, tq, d), lambda dd, c: (0, lat_idx(dd, c), 0)),
            pl.BlockSpec((n_b, tq, d), lambda dd, c: (0, ctx_idx(dd, c), 0)),
            wspec(bd_bre.shape[1:]), wspec(bd_bim.shape[1:]),
            wspec(bd_cre.shape[1:]), wspec(bd_cim.shape[1:]),
            wspec(a_re.shape[1:]), wspec(a_im.shape[1:]),
        ],
        out_specs=[
            pl.BlockSpec((None, n_b, tq, d), lambda dd, c: (dd, 0, lat_idx(dd, c), 0)),
            pl.BlockSpec((None, n_b, tq, d), lambda dd, c: (dd, 0, ctx_idx(dd, c), 0)),
        ],
        out_shape=[
            jax.ShapeDtypeStruct((2, n_b, s_len, d), F32),
            jax.ShapeDtypeStruct((2, n_b, l_len, d), F32),
        ],
        scratch_shapes=[
            pltpu.VMEM((n_b * pitch, d), F32),
            pltpu.VMEM((n_pairs, 2 * n_b * pitch, V7X_LANES), F32),
            pltpu.VMEM((n_pairs, 2 * n_b * pitch, V7X_LANES), F32),
            pltpu.VMEM((n_pairs, V7X_SUBLANES, V7X_LANES), F32),
            pltpu.VMEM((n_pairs, V7X_SUBLANES, V7X_LANES), F32),
        ],
        compiler_params=_cparams(("arbitrary", "arbitrary")),
        name="s5_scan",
    )(u_lat, u_ctx, bd_bre, bd_bim, bd_cre, bd_cim, a_re, a_im)


def _gelu_tanh(x):
    return 0.5 * x * (1.0 + jnp.tanh(math.sqrt(2.0 / math.pi) * (x + 0.044715 * (x * x * x))))


def _s5_out_kernel(yf_ref, yb_ref, u_ref, dsk_ref, h_ref, gt_ref, w1_ref, w2_ref, o_ref, z_ref):
    j = pl.program_id(1)

    @pl.when(j == 0)
    def _():
        y = yf_ref[...] + yb_ref[...] + dsk_ref[...] * u_ref[...]
        z_ref[...] = _gelu_tanh(y).astype(BF16)

    z = z_ref[...]
    a = jnp.dot(z, w1_ref[...].astype(BF16), preferred_element_type=F32)
    b = jnp.dot(z, w2_ref[...].astype(BF16), preferred_element_type=F32)
    o_ref[...] = h_ref[...] + gt_ref[...] * (a * jax.nn.sigmoid(b))


def _s5_out(y, u, h, mods, layer, gate_col, d_skip, w1, w2, sub, *, tm, bidx, tn=512):
    n_tok, d = h.shape
    return pl.pallas_call(
        _s5_out_kernel,
        grid=(n_tok // tm, d // tn),
        in_specs=[
            pl.BlockSpec((None, tm, d), lambda i, j: (0, i, 0)),
            pl.BlockSpec((None, tm, d), lambda i, j: (1, i, 0)),
            pl.BlockSpec((tm, d), lambda i, j: (i, 0)),
            pl.BlockSpec((None, 1, d), lambda i, j: (sub, 0, 0)),
            pl.BlockSpec((tm, tn), lambda i, j: (i, j)),
            pl.BlockSpec((None, None, 1, tn), lambda i, j: (layer, bidx(i), 0, gate_col * (d // tn) + j)),
            pl.BlockSpec((None, d, tn), lambda i, j: (sub, 0, j)),
            pl.BlockSpec((None, d, tn), lambda i, j: (sub, 0, j)),
        ],
        out_specs=pl.BlockSpec((tm, tn), lambda i, j: (i, j)),
        out_shape=jax.ShapeDtypeStruct((n_tok, d), F32),
        scratch_shapes=[pltpu.VMEM((tm, d), BF16)],
        compiler_params=_cparams(("arbitrary", "arbitrary")),
        name="s5_out",
    )(y, y, u, d_skip.reshape(d_skip.shape[0], 1, d), h, mods, w1, w2)


def _rope_tables(seq):
    pairs = HEAD_DIM // 4
    freqs = ROPE_THETA ** (-jnp.arange(pairs, dtype=F32) / pairs)
    pos = jnp.arange(seq, dtype=jnp.int32)
    ang_r = (pos // GRID_W).astype(F32)[:, None] * freqs
    ang_c = (pos % GRID_W).astype(F32)[:, None] * freqs
    cos = jnp.concatenate([jnp.cos(ang_r), jnp.cos(ang_r), jnp.cos(ang_c), jnp.cos(ang_c)], axis=-1)
    sin = jnp.concatenate([-jnp.sin(ang_r), jnp.sin(ang_r), -jnp.sin(ang_c), jnp.sin(ang_c)], axis=-1)
    return cos, sin


def _qkv_kernel(h_ref, nrm_ref, sh_ref, sc_ref, w_ref, qg_ref, kg_ref, cos_ref, sin_ref, o_ref, m_ref,
                *, col0, n_q_blk, n_k_blk, rope):
    j = pl.program_id(1)

    @pl.when(j == 0)
    def _():
        m_ref[...] = _norm_mod(h_ref[...], nrm_ref[...], sh_ref[...], sc_ref[...]).astype(BF16)

    acc = jnp.dot(m_ref[...], w_ref[...].astype(BF16), preferred_element_type=F32)
    heads = o_ref.shape[0]
    jj = j + col0
    is_v = jj >= n_q_blk + n_k_blk

    @pl.when(is_v)
    def _():
        for hh in range(heads):
            o_ref[hh] = acc[:, hh * HEAD_DIM:(hh + 1) * HEAD_DIM].astype(BF16)

    @pl.when(jnp.logical_not(is_v))
    def _():
        gain = jnp.where(jj < n_q_blk, qg_ref[...], kg_ref[...])
        lane = lax.broadcasted_iota(jnp.int32, (1, HEAD_DIM), 1)
        first = (lane % (HEAD_DIM // 2)) < (HEAD_DIM // 4)
        for hh in range(heads):
            x = acc[:, hh * HEAD_DIM:(hh + 1) * HEAD_DIM]
            r = lax.rsqrt(jnp.mean(x * x, axis=-1, keepdims=True) + EPS)
            xn = x * r * gain
            if rope:
                partner = jnp.where(first, pltpu.roll(xn, HEAD_DIM - HEAD_DIM // 4, 1),
                                    pltpu.roll(xn, HEAD_DIM // 4, 1))
                xn = xn * cos_ref[...] + partner * sin_ref[...]
            o_ref[hh] = xn.astype(BF16)


def _qkv(h, mods, layer, mod_base, norm_g, w_qkv, sub, q_gain, k_gain, cos, sin, *, seq, tm, bidx,
         col0, n_col, rope, tn=512):
    n_tok, d = h.shape
    n_b = n_tok // seq
    hpb = tn // HEAD_DIM
    q_dim = d
    kv_dim = (w_qkv.shape[2] - q_dim) // 2
    tiles_per_seq = seq // tm
    kern = functools.partial(_qkv_kernel, col0=col0, n_q_blk=q_dim // tn, n_k_blk=kv_dim // tn, rope=rope)
    return pl.pallas_call(
        kern,
        grid=(n_tok // tm, n_col),
        in_specs=[
            pl.BlockSpec((tm, d), lambda i, j: (i, 0)),
            pl.BlockSpec((None, 1, d), lambda i, j: (layer, 0, 0)),
            _mod_spec(layer, mod_base + 0, d, bidx),
            _mod_spec(layer, mod_base + 1, d, bidx),
            pl.BlockSpec((None, d, tn), lambda i, j: (sub, 0, j + col0)),
            pl.BlockSpec((None, 1, HEAD_DIM), lambda i, j: (sub, 0, 0)),
            pl.BlockSpec((None, 1, HEAD_DIM), lambda i, j: (sub, 0, 0)),
            pl.BlockSpec((tm, HEAD_DIM), lambda i, j: (i % tiles_per_seq, 0)),
            pl.BlockSpec((tm, HEAD_DIM), lambda i, j: (i % tiles_per_seq, 0)),
        ],
        out_specs=pl.BlockSpec((None, hpb, tm, HEAD_DIM),
                               lambda i, j: (i // tiles_per_seq, j, i % tiles_per_seq, 0)),
        out_shape=jax.ShapeDtypeStruct((n_b, n_col * hpb, seq, HEAD_DIM), BF16),
        scratch_shapes=[pltpu.VMEM((tm, d), BF16)],
        compiler_params=_cparams(("arbitrary", "arbitrary")),
        name="qkv",
    )(h, norm_g.reshape(norm_g.shape[0], 1, d), mods, mods, w_qkv,
      q_gain.reshape(q_gain.shape[0], 1, HEAD_DIM), k_gain.reshape(k_gain.shape[0], 1, HEAD_DIM), cos, sin)


def _attn_kernel(q_ref, kl_ref, vl_ref, kc_ref, vc_ref, o_ref):
    rep, tq, hd = q_ref.shape
    q = q_ref[...].reshape(rep * tq, hd)
    dn = (((1,), (1,)), ((), ()))
    s_l = lax.dot_general(q, kl_ref[...], dn, preferred_element_type=F32)
    s_c = lax.dot_general(q, kc_ref[...], dn, preferred_element_type=F32)
    m = jnp.maximum(jnp.max(s_l, axis=-1, keepdims=True), jnp.max(s_c, axis=-1, keepdims=True))
    scale = HEAD_DIM ** -0.5
    p_l = jnp.exp((s_l - m) * scale)
    p_c = jnp.exp((s_c - m) * scale)
    den = jnp.sum(p_l, axis=-1, keepdims=True) + jnp.sum(p_c, axis=-1, keepdims=True)
    o = (jnp.dot(p_l.astype(BF16), vl_ref[...], preferred_element_type=F32)
         + jnp.dot(p_c.astype(BF16), vc_ref[...], preferred_element_type=F32)) / den
    for r in range(rep):
        o_ref[:, r * hd:(r + 1) * hd] = o[r * tq:(r + 1) * tq].astype(BF16)


def _attention(qkv_lat, kv_ctx, n_q_heads, n_kv_heads, *, tq=256):
    n_b, _, seq, hd = qkv_lat.shape
    l_len = kv_ctx.shape[2]
    rep = n_q_heads // n_kv_heads
    return pl.pallas_call(
        _attn_kernel,
        grid=(n_b, n_kv_heads, seq // tq),
        in_specs=[
            pl.BlockSpec((None, rep, tq, hd), lambda b, g, i: (b, g, i, 0)),
            pl.BlockSpec((None, None, seq, hd), lambda b, g, i: (b, n_q_heads + g, 0, 0)),
            pl.BlockSpec((None, None, seq, hd), lambda b, g, i: (b, n_q_heads + n_kv_heads + g, 0, 0)),
            pl.BlockSpec((None, None, l_len, hd), lambda b, g, i: (b, g, 0, 0)),
            pl.BlockSpec((None, None, l_len, hd), lambda b, g, i: (b, n_kv_heads + g, 0, 0)),
        ],
        out_specs=pl.BlockSpec((None, tq, rep * hd), lambda b, g, i: (b, i, g)),
        out_shape=jax.ShapeDtypeStruct((n_b, seq, n_q_heads * hd), BF16),
        compiler_params=_cparams(("arbitrary", "arbitrary", "arbitrary")),
        name="attention",
    )(qkv_lat, qkv_lat, qkv_lat, kv_ctx, kv_ctx)


def _oproj_kernel(x_ref, w_ref, h_ref, gt_ref, o_ref):
    acc = jnp.dot(x_ref[...], w_ref[...].astype(BF16), preferred_element_type=F32)
    o_ref[...] = h_ref[...] + gt_ref[...] * acc


def _oproj(x, w_o, sub, h, mods, layer, gate_col, *, tm, bidx, tn=512):
    n_tok, d = h.shape
    k_dim = x.shape[1]
    return pl.pallas_call(
        _oproj_kernel,
        grid=(n_tok // tm, d // tn),
        in_specs=[
            pl.BlockSpec((tm, k_dim), lambda i, j: (i, 0)),
            pl.BlockSpec((None, k_dim, tn), lambda i, j: (sub, 0, j)),
            pl.BlockSpec((tm, tn), lambda i, j: (i, j)),
            pl.BlockSpec((None, None, 1, tn), lambda i, j: (layer, bidx(i), 0, gate_col * (d // tn) + j)),
        ],
        out_specs=pl.BlockSpec((tm, tn), lambda i, j: (i, j)),
        out_shape=jax.ShapeDtypeStruct((n_tok, d), F32),
        compiler_params=_cparams(("arbitrary", "arbitrary")),
        name="oproj",
    )(x, w_o, h, mods)


def kernel(x, c, ctx, c_ctx, ada_w, ada_b, norm_ffn1, norm_mix, norm_ffn2, ffn1_w_gate, ffn1_w_up, ffn1_w_down, ffn2_w_gate, ffn2_w_up, ffn2_w_down, s5_a_re, s5_a_im, s5_log_dt, s5_b_re, s5_b_im, s5_c_re, s5_c_im, s5_d, s5_glu_w1, s5_glu_w2, attn_w_qkv, attn_w_o, attn_q_gain, attn_k_gain):
    n_b, seq, d = x.shape
    l_len = ctx.shape[1]
    depth = ada_w.shape[0]
    assert depth == 2 and ada_w.shape[2] == N_MOD * d
    assert n_b == 4, "the S5 scan packs 4 sequences x 2 lane tiles onto the 8 sublanes"
    n_groups = d // S5_GROUP
    n_q_heads = d // HEAD_DIM
    n_kv_heads = n_q_heads // KV_REP

    tm_lat = min(1024, seq)
    tm_ctx = min(1024, n_b * l_len)
    ctx_row = n_b

    def bidx_lat(tm):
        return lambda i: (i * tm) // seq

    def bidx_ctx(tm):
        return lambda i: ctx_row

    cond = jnp.concatenate([c, c_ctx[None, :], jnp.zeros((16 - n_b - 1, d), F32)], axis=0)
    mods = _ada_mods(cond, ada_w, ada_b)
    mods = mods.reshape(depth, 16, 1, N_MOD * d)

    h_lat = x.reshape(n_b * seq, d)
    h_ctx = ctx.reshape(n_b * l_len, d)

    layer = 0
    ffn1 = (ffn1_w_gate, ffn1_w_up, ffn1_w_down)
    ffn2 = (ffn2_w_gate, ffn2_w_up, ffn2_w_down)
    h_lat = _ffn(h_lat, mods, layer, 0, norm_ffn1, *ffn1, tm=tm_lat, bidx=bidx_lat(tm_lat))
    h_ctx = _ffn(h_ctx, mods, layer, 0, norm_ffn1, *ffn1, tm=tm_ctx, bidx=bidx_ctx(tm_ctx))

    u_lat = _normmod(h_lat, mods, layer, 3, norm_mix, tm=tm_lat, bidx=bidx_lat(tm_lat))
    u_ctx = _normmod(h_ctx, mods, layer, 3, norm_mix, tm=tm_ctx, bidx=bidx_ctx(tm_ctx))

    lam_re, lam_im, bbar_re, bbar_im = _s5_discretize(s5_a_re[0], s5_a_im[0], s5_log_dt[0], s5_b_re[0], s5_b_im[0])
    bd_bre = _block_diag(jnp.swapaxes(bbar_re, -1, -2), S5_GB).astype(BF16)
    bd_bim = _block_diag(jnp.swapaxes(bbar_im, -1, -2), S5_GB).astype(BF16)
    bd_cre = _block_diag(jnp.swapaxes(s5_c_re[0], -1, -2), S5_GB).astype(BF16)
    bd_cim = _block_diag(jnp.swapaxes(-s5_c_im[0], -1, -2), S5_GB).astype(BF16)
    n_pairs = n_groups * S5_STATE // V7X_LANES // 2
    a_re = _pair_rows(lam_re, n_pairs)
    a_im = _pair_rows(lam_im, n_pairs)

    y_lat, y_ctx = _s5_scan(u_lat.reshape(n_b, seq, d), u_ctx.reshape(n_b, l_len, d),
                            bd_bre, bd_bim, bd_cre, bd_cim, a_re, a_im)
    tm_glu = min(512, seq)
    tm_glu_c = min(512, n_b * l_len)
    h_lat = _s5_out(y_lat.reshape(2, n_b * seq, d), u_lat, h_lat, mods, layer, 5, s5_d, s5_glu_w1, s5_glu_w2, 0,
                    tm=tm_glu, bidx=bidx_lat(tm_glu))
    h_ctx = _s5_out(y_ctx.reshape(2, n_b * l_len, d), u_ctx, h_ctx, mods, layer, 5, s5_d, s5_glu_w1, s5_glu_w2, 0,
                    tm=tm_glu_c, bidx=bidx_ctx(tm_glu_c))

    h_lat = _ffn(h_lat, mods, layer, 6, norm_ffn2, *ffn2, tm=tm_lat, bidx=bidx_lat(tm_lat))
    h_ctx = _ffn(h_ctx, mods, layer, 6, norm_ffn2, *ffn2, tm=tm_ctx, bidx=bidx_ctx(tm_ctx))

    layer = 1
    h_lat = _ffn(h_lat, mods, layer, 0, norm_ffn1, *ffn1, tm=tm_lat, bidx=bidx_lat(tm_lat))
    h_ctx = _ffn(h_ctx, mods, layer, 0, norm_ffn1, *ffn1, tm=tm_ctx, bidx=bidx_ctx(tm_ctx))

    cos, sin = _rope_tables(seq)
    tn_qkv = 512
    n_cols = attn_w_qkv.shape[2] // tn_qkv
    n_q_cols = d // tn_qkv
    tm_qkv = min(512, seq)
    tm_qkv_c = min(512, l_len)
    qkv_lat = _qkv(h_lat, mods, layer, 3, norm_mix, attn_w_qkv, 0, attn_q_gain, attn_k_gain, cos, sin,
                   seq=seq, tm=tm_qkv, bidx=bidx_lat(tm_qkv), col0=0, n_col=n_cols, rope=True, tn=tn_qkv)
    kv_ctx = _qkv(h_ctx, mods, layer, 3, norm_mix, attn_w_qkv, 0, attn_q_gain, attn_k_gain,
                  cos[:l_len] if l_len <= seq else jnp.zeros((l_len, HEAD_DIM), F32),
                  sin[:l_len] if l_len <= seq else jnp.zeros((l_len, HEAD_DIM), F32),
                  seq=l_len, tm=tm_qkv_c, bidx=bidx_ctx(tm_qkv_c), col0=n_q_cols, n_col=n_cols - n_q_cols,
                  rope=False, tn=tn_qkv)
    o_lat = _attention(qkv_lat, kv_ctx, n_q_heads, n_kv_heads, tq=min(256, seq))
    h_lat = _oproj(o_lat.reshape(n_b * seq, d), attn_w_o, 0, h_lat, mods, layer, 5,
                   tm=tm_lat, bidx=bidx_lat(tm_lat))

    h_lat = _ffn(h_lat, mods, layer, 6, norm_ffn2, *ffn2, tm=tm_lat, bidx=bidx_lat(tm_lat))
    return h_lat.reshape(n_b, seq, d)
```

```python
import functools
import math

import jax
import jax.numpy as jnp
from jax import lax
from jax.experimental import pallas as pl
from jax.experimental.pallas import tpu as pltpu

F32 = jnp.float32
BF16 = jnp.bfloat16

EPS = 1e-6
N_MOD = 9
GRID_W = 64
HEAD_DIM = 128
KV_REP = 4
ROPE_THETA = 10000.0
S5_GROUP = 16
S5_STATE = 64

V7X_LANES = 128
V7X_SUBLANES = 8
V7X_VMEM_LIMIT = 60 * 1024 * 1024

S5_GB = 8
S5_TQ = 64
S5_PITCH = 68
COND_ROWS = 16
COL_BLK = 512


def _cparams(sem):
    return pltpu.CompilerParams(dimension_semantics=sem, vmem_limit_bytes=V7X_VMEM_LIMIT)


def _resident(shape, index):
    return pl.BlockSpec(shape, index, pipeline_mode=pl.Buffered(1))


def _ada_kernel(a_ref, w_ref, b_ref, o_ref):
    a = a_ref[...]
    s = (a * jax.nn.sigmoid(a)).astype(BF16)
    o_ref[...] = jnp.dot(s, w_ref[...].astype(BF16), preferred_element_type=F32) + b_ref[...]


def _ada_mods(cond, ada_w, ada_b, tn=1024):
    depth, d, n = ada_w.shape
    rows = cond.shape[0]
    return pl.pallas_call(
        _ada_kernel,
        grid=(depth, n // tn),
        in_specs=[
            pl.BlockSpec((rows, d), lambda l, j: (0, 0)),
            pl.BlockSpec((None, d, tn), lambda l, j: (l, 0, j)),
            pl.BlockSpec((None, 1, tn), lambda l, j: (l, 0, j)),
        ],
        out_specs=pl.BlockSpec((None, rows, tn), lambda l, j: (l, 0, j)),
        out_shape=jax.ShapeDtypeStruct((depth, rows, n), F32),
        compiler_params=_cparams(("arbitrary", "arbitrary")),
        name="ada_mods",
    )(cond, ada_w, ada_b.reshape(depth, 1, n))


def _mod_spec(layer, col, d, bidx):
    return pl.BlockSpec((None, None, 1, d), lambda i, *_: (layer, bidx(i), 0, col))


def _norm_mod(x, g, shift, scale):
    r = lax.rsqrt(jnp.mean(x * x, axis=-1, keepdims=True) + EPS)
    return (x * r * g) * (1.0 + scale) + shift


def _ffn_kernel(h_ref, nrm_ref, sh_ref, sc_ref, gt_ref, wg_ref, wu_ref, wd_ref, o_ref, m_ref):
    f = pl.program_id(1)

    @pl.when(f == 0)
    def _():
        x = h_ref[...]
        m_ref[...] = _norm_mod(x, nrm_ref[...], sh_ref[...], sc_ref[...]).astype(BF16)
        o_ref[...] = x

    m = m_ref[...]
    g = jnp.dot(m, wg_ref[...], preferred_element_type=F32)
    u = jnp.dot(m, wu_ref[...], preferred_element_type=F32)
    a = (g * jax.nn.sigmoid(g) * u).astype(BF16)
    for nb in range(o_ref.shape[1] // COL_BLK):
        cols = slice(nb * COL_BLK, (nb + 1) * COL_BLK)
        p = jnp.dot(a, wd_ref[:, cols], preferred_element_type=F32)
        o_ref[:, cols] += (0.5 * gt_ref[:, cols]) * p


def _ffn(h, mods, layer, mod_base, norm_g, w_gate, w_up, w_down, *, tm, bidx, tf=512):
    n_tok, d = h.shape
    f_dim = w_gate.shape[2]
    return pl.pallas_call(
        _ffn_kernel,
        grid=(n_tok // tm, f_dim // tf),
        in_specs=[
            pl.BlockSpec((tm, d), lambda i, f: (i, 0), pipeline_mode=pl.Buffered(1)),
            pl.BlockSpec((None, 1, d), lambda i, f: (layer, 0, 0)),
            _mod_spec(layer, mod_base + 0, d, bidx),
            _mod_spec(layer, mod_base + 1, d, bidx),
            _mod_spec(layer, mod_base + 2, d, bidx),
            pl.BlockSpec((None, d, tf), lambda i, f: (layer, 0, f)),
            pl.BlockSpec((None, d, tf), lambda i, f: (layer, 0, f)),
            pl.BlockSpec((None, tf, d), lambda i, f: (layer, f, 0)),
        ],
        out_specs=pl.BlockSpec((tm, d), lambda i, f: (i, 0)),
        out_shape=jax.ShapeDtypeStruct((n_tok, d), F32),
        scratch_shapes=[pltpu.VMEM((tm, d), BF16)],
        compiler_params=_cparams(("arbitrary", "arbitrary")),
        name="ffn",
    )(h, norm_g.reshape(norm_g.shape[0], 1, d), mods, mods, mods, w_gate, w_up, w_down)


def _zoh(a_re, a_im, log_dt):
    dt = jnp.exp(log_dt)
    mag = jnp.exp(a_re * dt)
    l_re = mag * jnp.cos(a_im * dt)
    l_im = mag * jnp.sin(a_im * dt)
    den = a_re * a_re + a_im * a_im
    c_re = ((l_re - 1.0) * a_re + l_im * a_im) / den
    c_im = (l_im * a_re - (l_re - 1.0) * a_im) / den
    return l_re, l_im, c_re, c_im


def _lane_tile(x, reps):
    w = x.shape[1]
    row = lax.broadcasted_iota(jnp.int32, (w, w * reps), 0)
    col = lax.broadcasted_iota(jnp.int32, (w, w * reps), 1)
    sel = (row == col % w).astype(BF16)
    return jnp.dot(x, sel, preferred_element_type=F32)


def _s5_param_kernel(are_ref, aim_ref, ldt_ref, bre_ref, bim_ref, cre_ref, cim_ref, ar2_ref, ai2_ref, ld2_ref,
                     obr_ref, obi_ref, ocr_ref, oci_ref, olr_ref, oli_ref):
    gpb = are_ref.shape[0] // S5_GROUP
    _, _, k_re, k_im = _zoh(are_ref[...], aim_ref[...], ldt_ref[...])
    bb_re = k_re * bre_ref[...] - k_im * bim_ref[...]
    bb_im = k_re * bim_ref[...] + k_im * bre_ref[...]
    n_in, n_st = bb_re.shape[0], S5_STATE * gpb
    on_diag = (lax.broadcasted_iota(jnp.int32, (n_in, n_st), 0) // S5_GROUP
               == lax.broadcasted_iota(jnp.int32, (n_in, n_st), 1) // S5_STATE)
    obr_ref[...] = jnp.where(on_diag, _lane_tile(bb_re.astype(BF16), gpb), 0.0).astype(BF16)
    obi_ref[...] = jnp.where(on_diag, _lane_tile(bb_im.astype(BF16), gpb), 0.0).astype(BF16)
    on_diag_t = (lax.broadcasted_iota(jnp.int32, (n_st, n_in), 0) // S5_STATE
                 == lax.broadcasted_iota(jnp.int32, (n_st, n_in), 1) // S5_GROUP)
    ocr_ref[...] = jnp.where(on_diag_t, _lane_tile(cre_ref[...].astype(BF16), gpb), 0.0).astype(BF16)
    oci_ref[...] = jnp.where(on_diag_t, _lane_tile((-cim_ref[...]).astype(BF16), gpb), 0.0).astype(BF16)
    l_re, l_im, _, _ = _zoh(ar2_ref[...], ai2_ref[...], ld2_ref[...])
    olr_ref[...] = l_re
    oli_ref[...] = l_im


def _s5_params(a_re, a_im, log_dt, b_re, b_im, c_re, c_im):
    dirs, g, p = a_re.shape
    hch = b_re.shape[3]
    gpb = g // S5_GB
    rows_in, rows_st = gpb * hch, gpb * p

    def per_channel(v):
        return jnp.broadcast_to(v[:, :, None, :], (dirs, g, hch, p)).reshape(dirs, S5_GB, rows_in, p)

    ldt = jnp.broadcast_to(log_dt[:, :, None], (dirs, g, p))
    args = (
        per_channel(a_re), per_channel(a_im), per_channel(ldt),
        jnp.swapaxes(b_re, 2, 3).reshape(dirs, S5_GB, rows_in, p),
        jnp.swapaxes(b_im, 2, 3).reshape(dirs, S5_GB, rows_in, p),
        jnp.swapaxes(c_re, 2, 3).reshape(dirs, S5_GB, rows_st, hch),
        jnp.swapaxes(c_im, 2, 3).reshape(dirs, S5_GB, rows_st, hch),
        a_re.reshape(dirs, g * p // V7X_LANES, V7X_LANES),
        a_im.reshape(dirs, g * p // V7X_LANES, V7X_LANES),
        ldt.reshape(dirs, g * p // V7X_LANES, V7X_LANES),
    )
    blk = lambda r, c: pl.BlockSpec((None, None, r, c), lambda dd, gb: (dd, gb, 0, 0))
    flat = pl.BlockSpec((None, g * p // V7X_LANES, V7X_LANES), lambda dd, gb: (dd, 0, 0))
    return pl.pallas_call(
        _s5_param_kernel,
        grid=(dirs, S5_GB),
        in_specs=[blk(rows_in, p)] * 5 + [blk(rows_st, hch)] * 2 + [flat] * 3,
        out_specs=[blk(rows_in, rows_st), blk(rows_in, rows_st), blk(rows_st, rows_in), blk(rows_st, rows_in),
                   flat, flat],
        out_shape=[
            jax.ShapeDtypeStruct((dirs, S5_GB, rows_in, rows_st), BF16),
            jax.ShapeDtypeStruct((dirs, S5_GB, rows_in, rows_st), BF16),
            jax.ShapeDtypeStruct((dirs, S5_GB, rows_st, rows_in), BF16),
            jax.ShapeDtypeStruct((dirs, S5_GB, rows_st, rows_in), BF16),
            jax.ShapeDtypeStruct((dirs, g * p // V7X_LANES, V7X_LANES), F32),
            jax.ShapeDtypeStruct((dirs, g * p // V7X_LANES, V7X_LANES), F32),
        ],
        compiler_params=_cparams(("arbitrary", "arbitrary")),
        name="s5_params",
    )(*args)


def _pair_rows(lam):
    dirs, tiles, lanes = lam.shape
    half = V7X_SUBLANES // 2
    x = jnp.broadcast_to(lam.reshape(dirs, 2, tiles // 2, 1, lanes), (dirs, 2, tiles // 2, half, lanes))
    return jnp.transpose(x, (0, 2, 1, 3, 4)).reshape(dirs, tiles // 2, V7X_SUBLANES, lanes)


def _s5_scan_kernel(*refs, n_ctx_chunks, tq, pitch, reverse, ctx_row, emit_z):
    if emit_z:
        (hl_ref, hc_ref, nrm_ref, sh_ref, sc_ref, bre_ref, bim_ref, cre_ref, cim_ref, are_ref, aim_ref,
         yl_ref, yc_ref, dsk_ref, ol_ref, oc_ref, u_st, s_re, s_im, h_re, h_im) = refs
    else:
        (hl_ref, hc_ref, nrm_ref, sh_ref, sc_ref, bre_ref, bim_ref, cre_ref, cim_ref, are_ref, aim_ref,
         ol_ref, oc_ref, u_st, s_re, s_im, h_re, h_im) = refs
    c = pl.program_id(0)
    n_b = hl_ref.shape[0]
    half_rows = n_b * pitch
    n_pairs = s_re.shape[0]
    tiles_per_gb = 2 * n_pairs // S5_GB
    gb_per_half = S5_GB // 2
    lanes_in = u_st.shape[1] // S5_GB

    @pl.when(c == 0)
    def _():
        h_re[...] = jnp.zeros_like(h_re)
        h_im[...] = jnp.zeros_like(h_im)
        u_st[...] = jnp.zeros_like(u_st)

    is_ctx = c < n_ctx_chunks

    @pl.when(is_ctx)
    def _():
        for b in range(n_b):
            u_st[b * pitch:b * pitch + tq, :] = _norm_mod(hc_ref[b], nrm_ref[...], sh_ref[ctx_row], sc_ref[ctx_row])

    @pl.when(jnp.logical_not(is_ctx))
    def _():
        for b in range(n_b):
            u_st[b * pitch:b * pitch + tq, :] = _norm_mod(hl_ref[b], nrm_ref[...], sh_ref[b], sc_ref[b])

    for gb in range(S5_GB):
        lhs = u_st[:, gb * lanes_in:(gb + 1) * lanes_in].astype(BF16)
        p_re = jnp.dot(lhs, bre_ref[gb], preferred_element_type=F32)
        p_im = jnp.dot(lhs, bim_ref[gb], preferred_element_type=F32)
        half = gb // gb_per_half
        rows = slice(half * half_rows, (half + 1) * half_rows)
        for j in range(tiles_per_gb):
            n = (gb % gb_per_half) * tiles_per_gb + j
            s_re[n, rows, :] = p_re[:, j * V7X_LANES:(j + 1) * V7X_LANES]
            s_im[n, rows, :] = p_im[:, j * V7X_LANES:(j + 1) * V7X_LANES]

    per_pass = 8
    for k in range(n_pairs // per_pass):
        ns = list(range(k * per_pass, (k + 1) * per_pass))
        a_r = [are_ref[n] for n in ns]
        a_i = [aim_ref[n] for n in ns]
        init = (tuple(h_re[n] for n in ns), tuple(h_im[n] for n in ns))

        def body(t, carry, ns=ns, a_r=a_r, a_i=a_i):
            tt = tq - 1 - t if reverse else t
            hr, hi = carry
            out_r, out_i = [], []
            for idx, n in enumerate(ns):
                rows = pl.ds(tt, V7X_SUBLANES, stride=pitch)
                n_r = a_r[idx] * hr[idx] - a_i[idx] * hi[idx] + s_re[n, rows, :]
                n_i = a_r[idx] * hi[idx] + a_i[idx] * hr[idx] + s_im[n, rows, :]
                s_re[n, rows, :] = n_r
                s_im[n, rows, :] = n_i
                out_r.append(n_r)
                out_i.append(n_i)
            return tuple(out_r), tuple(out_i)

        fin_r, fin_i = lax.fori_loop(0, tq, body, init, unroll=2)
        for idx, n in enumerate(ns):
            h_re[n] = fin_r[idx]
            h_im[n] = fin_i[idx]

    ys = []
    for gb in range(S5_GB):
        half = gb // gb_per_half
        rows = slice(half * half_rows, (half + 1) * half_rows)
        base = (gb % gb_per_half) * tiles_per_gb
        l_re = jnp.concatenate([s_re[base + j, rows, :] for j in range(tiles_per_gb)], axis=1).astype(BF16)
        l_im = jnp.concatenate([s_im[base + j, rows, :] for j in range(tiles_per_gb)], axis=1).astype(BF16)
        ys.append(jnp.dot(l_re, cre_ref[gb], preferred_element_type=F32)
                  + jnp.dot(l_im, cim_ref[gb], preferred_element_type=F32))

    def emit(out_ref, yf_ref):
        for gb in range(S5_GB):
            cols = slice(gb * lanes_in, (gb + 1) * lanes_in)
            for b in range(n_b):
                y = ys[gb][b * pitch:b * pitch + tq]
                if emit_z:
                    y = y + yf_ref[b, :, cols] + dsk_ref[:, cols] * u_st[b * pitch:b * pitch + tq, cols]
                    out_ref[b, :, cols] = _gelu_tanh(y).astype(out_ref.dtype)
                else:
                    out_ref[b, :, cols] = y

    @pl.when(is_ctx)
    def _():
        emit(oc_ref, yc_ref if emit_z else None)

    @pl.when(jnp.logical_not(is_ctx))
    def _():
        emit(ol_ref, yl_ref if emit_z else None)


def _gelu_tanh(x):
    return 0.5 * x * (1.0 + jnp.tanh(math.sqrt(2.0 / math.pi) * (x + 0.044715 * (x * x * x))))


def _s5_scan(h_lat, h_ctx, mods, layer, mod_base, norm_g, maps, direction, y_prev=None, d_skip=None):
    n_b, s_len, d = h_lat.shape
    l_len = h_ctx.shape[1]
    tq, pitch = S5_TQ, S5_PITCH
    n_l, n_s = l_len // tq, s_len // tq
    bd_bre, bd_bim, bd_cre, bd_cim, a_re, a_im = maps
    n_pairs = a_re.shape[1]
    reverse = direction == 1
    emit_z = y_prev is not None

    def lat_idx(c):
        k = jnp.maximum(c - n_l, 0)
        return n_s - 1 - k if reverse else k

    def ctx_idx(c):
        k = jnp.minimum(c, n_l - 1)
        return n_l - 1 - k if reverse else k

    def wspec(arr):
        shape = arr.shape[1:]
        return _resident((None,) + shape, lambda c: (direction,) + (0,) * len(shape))

    lat_spec = pl.BlockSpec((n_b, tq, d), lambda c: (0, lat_idx(c), 0))
    ctx_spec = pl.BlockSpec((n_b, tq, d), lambda c: (0, ctx_idx(c), 0))
    mod_rows = lambda col: pl.BlockSpec((None, COND_ROWS, 1, d), lambda c: (layer, 0, 0, col))
    in_specs = [lat_spec, ctx_spec,
                pl.BlockSpec((None, 1, d), lambda c: (layer, 0, 0)),
                mod_rows(mod_base), mod_rows(mod_base + 1),
                wspec(bd_bre), wspec(bd_bim), wspec(bd_cre), wspec(bd_cim), wspec(a_re), wspec(a_im)]
    args = [h_lat, h_ctx, norm_g.reshape(norm_g.shape[0], 1, d), mods, mods,
            bd_bre, bd_bim, bd_cre, bd_cim, a_re, a_im]
    out_dtype = F32
    if emit_z:
        in_specs += [lat_spec, ctx_spec, pl.BlockSpec((None, 1, d), lambda c: (0, 0, 0))]
        args += [y_prev[0], y_prev[1], d_skip.reshape(d_skip.shape[0], 1, d)]
        out_dtype = BF16
    kern = functools.partial(_s5_scan_kernel, n_ctx_chunks=n_l, tq=tq, pitch=pitch, reverse=reverse,
                             ctx_row=n_b, emit_z=emit_z)
    return pl.pallas_call(
        kern,
        grid=(n_l + n_s,),
        in_specs=in_specs,
        out_specs=[lat_spec, ctx_spec],
        out_shape=[jax.ShapeDtypeStruct((n_b, s_len, d), out_dtype),
                   jax.ShapeDtypeStruct((n_b, l_len, d), out_dtype)],
        scratch_shapes=[
            pltpu.VMEM((n_b * pitch, d), F32),
            pltpu.VMEM((n_pairs, 2 * n_b * pitch, V7X_LANES), F32),
            pltpu.VMEM((n_pairs, 2 * n_b * pitch, V7X_LANES), F32),
            pltpu.VMEM((n_pairs, V7X_SUBLANES, V7X_LANES), F32),
            pltpu.VMEM((n_pairs, V7X_SUBLANES, V7X_LANES), F32),
        ],
        compiler_params=_cparams(("arbitrary",)),
        name="s5_bwd" if reverse else "s5_fwd",
    )(*args)


def _glu_kernel(z_ref, w1_ref, w2_ref, h_ref, gt_ref, o_ref):
    z = z_ref[...]
    for nb in range(o_ref.shape[1] // COL_BLK):
        cols = slice(nb * COL_BLK, (nb + 1) * COL_BLK)
        a = jnp.dot(z, w1_ref[:, cols], preferred_element_type=F32)
        b = jnp.dot(z, w2_ref[:, cols], preferred_element_type=F32)
        o_ref[:, cols] = h_ref[:, cols] + gt_ref[:, cols] * (a * jax.nn.sigmoid(b))


def _glu(z, w1, w2, sub, h, mods, layer, gate_col, *, tm, bidx):
    n_tok, d = h.shape
    return pl.pallas_call(
        _glu_kernel,
        grid=(n_tok // tm,),
        in_specs=[
            pl.BlockSpec((tm, d), lambda i: (i, 0)),
            _resident((None, d, d), lambda i: (sub, 0, 0)),
            _resident((None, d, d), lambda i: (sub, 0, 0)),
            pl.BlockSpec((tm, d), lambda i: (i, 0)),
            _mod_spec(layer, gate_col, d, bidx),
        ],
        out_specs=pl.BlockSpec((tm, d), lambda i: (i, 0)),
        out_shape=jax.ShapeDtypeStruct((n_tok, d), F32),
        compiler_params=_cparams(("arbitrary",)),
        name="s5_glu",
    )(z, w1, w2, h, mods)


def _rope_tables(seq):
    pairs = HEAD_DIM // 4
    freqs = ROPE_THETA ** (-jnp.arange(pairs, dtype=F32) / pairs)
    pos = jnp.arange(seq, dtype=jnp.int32)
    ang_r = (pos // GRID_W).astype(F32)[:, None] * freqs
    ang_c = (pos % GRID_W).astype(F32)[:, None] * freqs
    cos = jnp.concatenate([jnp.cos(ang_r), jnp.cos(ang_r), jnp.cos(ang_c), jnp.cos(ang_c)], axis=-1)
    sin = jnp.concatenate([-jnp.sin(ang_r), jnp.sin(ang_r), -jnp.sin(ang_c), jnp.sin(ang_c)], axis=-1)
    return cos, sin


def _qkv_kernel(h_ref, nrm_ref, sh_ref, sc_ref, w_ref, qg_ref, kg_ref, cos_ref, sin_ref, o_ref,
                *, head0, n_q_heads, n_kv_heads, rope):
    m = _norm_mod(h_ref[...], nrm_ref[...], sh_ref[...], sc_ref[...]).astype(BF16)
    lane = lax.broadcasted_iota(jnp.int32, (1, HEAD_DIM), 1)
    first = (lane % (HEAD_DIM // 2)) < (HEAD_DIM // 4)
    hpb = COL_BLK // HEAD_DIM
    for nb in range(w_ref.shape[1] // COL_BLK):
        acc = jnp.dot(m, w_ref[:, nb * COL_BLK:(nb + 1) * COL_BLK], preferred_element_type=F32)
        for hh in range(hpb):
            head = head0 + nb * hpb + hh
            x = acc[:, hh * HEAD_DIM:(hh + 1) * HEAD_DIM]
            if head < n_q_heads + n_kv_heads:
                gain = qg_ref[...] if head < n_q_heads else kg_ref[...]
                xn = x * lax.rsqrt(jnp.mean(x * x, axis=-1, keepdims=True) + EPS) * gain
                if rope:
                    partner = jnp.where(first, pltpu.roll(xn, HEAD_DIM - HEAD_DIM // 4, 1),
                                        pltpu.roll(xn, HEAD_DIM // 4, 1))
                    xn = xn * cos_ref[...] + partner * sin_ref[...]
                x = xn
            o_ref[nb * hpb + hh] = x.astype(BF16)


def _qkv(h, mods, layer, mod_base, norm_g, w_qkv, sub, q_gain, k_gain, cos, sin, *, seq, tm, bidx,
         n_q_heads, n_kv_heads, kv_only, rope):
    n_tok, d = h.shape
    n_b = n_tok // seq
    n_cols = w_qkv.shape[2]
    head0 = n_q_heads if kv_only else 0
    width = n_cols - head0 * HEAD_DIM
    heads = width // HEAD_DIM
    assert (head0 * HEAD_DIM) % width == 0
    tiles_per_seq = seq // tm
    kern = functools.partial(_qkv_kernel, head0=head0, n_q_heads=n_q_heads, n_kv_heads=n_kv_heads, rope=rope)
    return pl.pallas_call(
        kern,
        grid=(n_tok // tm,),
        in_specs=[
            pl.BlockSpec((tm, d), lambda i: (i, 0)),
            pl.BlockSpec((None, 1, d), lambda i: (layer, 0, 0)),
            _mod_spec(layer, mod_base + 0, d, bidx),
            _mod_spec(layer, mod_base + 1, d, bidx),
            _resident((None, d, width), lambda i: (sub, 0, head0 * HEAD_DIM // width)),
            pl.BlockSpec((None, 1, HEAD_DIM), lambda i: (sub, 0, 0)),
            pl.BlockSpec((None, 1, HEAD_DIM), lambda i: (sub, 0, 0)),
            pl.BlockSpec((tm, HEAD_DIM), lambda i: (i % tiles_per_seq, 0)),
            pl.BlockSpec((tm, HEAD_DIM), lambda i: (i % tiles_per_seq, 0)),
        ],
        out_specs=pl.BlockSpec((None, heads, tm, HEAD_DIM), lambda i: (i // tiles_per_seq, 0, i % tiles_per_seq, 0)),
        out_shape=jax.ShapeDtypeStruct((n_b, heads, seq, HEAD_DIM), BF16),
        compiler_params=_cparams(("arbitrary",)),
        name="qkv",
    )(h, norm_g.reshape(norm_g.shape[0], 1, d), mods, mods, w_qkv,
      q_gain.reshape(q_gain.shape[0], 1, HEAD_DIM), k_gain.reshape(k_gain.shape[0], 1, HEAD_DIM), cos, sin)


def _attn_kernel(q_ref, kl_ref, vl_ref, kc_ref, vc_ref, o_ref):
    rep, tq, hd = q_ref.shape
    dn = (((1,), (1,)), ((), ()))
    c = (HEAD_DIM ** -0.5) * math.log2(math.e)
    for r in range(rep):
        q = q_ref[r]
        s_l = lax.dot_general(q, kl_ref[...], dn, preferred_element_type=F32)
        s_c = lax.dot_general(q, kc_ref[...], dn, preferred_element_type=F32)
        m = jnp.maximum(jnp.max(s_l, axis=-1, keepdims=True), jnp.max(s_c, axis=-1, keepdims=True))
        p_l = jnp.exp2((s_l - m) * c)
        p_c = jnp.exp2((s_c - m) * c)
        den = jnp.sum(p_l, axis=-1, keepdims=True) + jnp.sum(p_c, axis=-1, keepdims=True)
        o = (jnp.dot(p_l.astype(BF16), vl_ref[...], preferred_element_type=F32)
             + jnp.dot(p_c.astype(BF16), vc_ref[...], preferred_element_type=F32)) / den
        o_ref[:, r * hd:(r + 1) * hd] = o.astype(BF16)


def _attention(qkv_lat, kv_ctx, n_q_heads, n_kv_heads, *, tq=256):
    n_b, _, seq, hd = qkv_lat.shape
    l_len = kv_ctx.shape[2]
    rep = n_q_heads // n_kv_heads
    return pl.pallas_call(
        _attn_kernel,
        grid=(n_b, n_kv_heads, seq // tq),
        in_specs=[
            pl.BlockSpec((None, rep, tq, hd), lambda b, g, i: (b, g, i, 0)),
            pl.BlockSpec((None, None, seq, hd), lambda b, g, i: (b, n_q_heads + g, 0, 0)),
            pl.BlockSpec((None, None, seq, hd), lambda b, g, i: (b, n_q_heads + n_kv_heads + g, 0, 0)),
            pl.BlockSpec((None, None, l_len, hd), lambda b, g, i: (b, g, 0, 0)),
            pl.BlockSpec((None, None, l_len, hd), lambda b, g, i: (b, n_kv_heads + g, 0, 0)),
        ],
        out_specs=pl.BlockSpec((None, tq, rep * hd), lambda b, g, i: (b, i, g)),
        out_shape=jax.ShapeDtypeStruct((n_b, seq, n_q_heads * hd), BF16),
        compiler_params=_cparams(("arbitrary", "arbitrary", "arbitrary")),
        name="attention",
    )(qkv_lat, qkv_lat, qkv_lat, kv_ctx, kv_ctx)


def _oproj_kernel(x_ref, w_ref, h_ref, gt_ref, o_ref):
    x = x_ref[...]
    for nb in range(o_ref.shape[1] // COL_BLK):
        cols = slice(nb * COL_BLK, (nb + 1) * COL_BLK)
        acc = jnp.dot(x, w_ref[:, cols], preferred_element_type=F32)
        o_ref[:, cols] = h_ref[:, cols] + gt_ref[:, cols] * acc


def _oproj(x, w_o, sub, h, mods, layer, gate_col, *, tm, bidx):
    n_tok, d = h.shape
    k_dim = x.shape[1]
    return pl.pallas_call(
        _oproj_kernel,
        grid=(n_tok // tm,),
        in_specs=[
            pl.BlockSpec((tm, k_dim), lambda i: (i, 0)),
            _resident((None, k_dim, d), lambda i: (sub, 0, 0)),
            pl.BlockSpec((tm, d), lambda i: (i, 0)),
            _mod_spec(layer, gate_col, d, bidx),
        ],
        out_specs=pl.BlockSpec((tm, d), lambda i: (i, 0)),
        out_shape=jax.ShapeDtypeStruct((n_tok, d), F32),
        compiler_params=_cparams(("arbitrary",)),
        name="oproj",
    )(x, w_o, h, mods)


def kernel(x, c, ctx, c_ctx, ada_w, ada_b, norm_ffn1, norm_mix, norm_ffn2, ffn1_w_gate, ffn1_w_up, ffn1_w_down, ffn2_w_gate, ffn2_w_up, ffn2_w_down, s5_a_re, s5_a_im, s5_log_dt, s5_b_re, s5_b_im, s5_c_re, s5_c_im, s5_d, s5_glu_w1, s5_glu_w2, attn_w_qkv, attn_w_o, attn_q_gain, attn_k_gain):
    n_b, seq, d = x.shape
    l_len = ctx.shape[1]
    depth = ada_w.shape[0]
    assert depth == 2 and ada_w.shape[2] == N_MOD * d
    assert n_b == 4, "the S5 scan packs 4 sequences x 2 lane tiles onto the 8 sublanes"
    n_q_heads = d // HEAD_DIM
    n_kv_heads = n_q_heads // KV_REP

    tm_lat = min(1024, seq)
    tm_ctx = min(1024, n_b * l_len)
    tm_mm = min(512, seq)
    tm_mm_ctx = min(512, l_len)
    ctx_row = n_b
    bidx_lat = lambda tm: (lambda i: (i * tm) // seq)
    bidx_ctx = lambda i: ctx_row

    cond = jnp.concatenate([c, c_ctx[None, :], jnp.zeros((COND_ROWS - n_b - 1, d), F32)], axis=0)
    mods = _ada_mods(cond, ada_w, ada_b).reshape(depth, COND_ROWS, 1, N_MOD * d)

    ffn1 = tuple(w.astype(BF16) for w in (ffn1_w_gate, ffn1_w_up, ffn1_w_down))
    ffn2 = tuple(w.astype(BF16) for w in (ffn2_w_gate, ffn2_w_up, ffn2_w_down))

    h_lat = x.reshape(n_b * seq, d)
    h_ctx = ctx.reshape(n_b * l_len, d)

    layer = 0
    h_lat = _ffn(h_lat, mods, layer, 0, norm_ffn1, *ffn1, tm=tm_lat, bidx=bidx_lat(tm_lat))
    h_ctx = _ffn(h_ctx, mods, layer, 0, norm_ffn1, *ffn1, tm=tm_ctx, bidx=bidx_ctx)

    params = _s5_params(s5_a_re[0], s5_a_im[0], s5_log_dt[0], s5_b_re[0], s5_b_im[0], s5_c_re[0], s5_c_im[0])
    maps = tuple(params[:4]) + (_pair_rows(params[4]), _pair_rows(params[5]))
    h3_lat, h3_ctx = h_lat.reshape(n_b, seq, d), h_ctx.reshape(n_b, l_len, d)
    y_fwd = _s5_scan(h3_lat, h3_ctx, mods, layer, 3, norm_mix, maps, 0)
    z_lat, z_ctx = _s5_scan(h3_lat, h3_ctx, mods, layer, 3, norm_mix, maps, 1, y_prev=y_fwd, d_skip=s5_d)
    glu_w1, glu_w2 = s5_glu_w1.astype(BF16), s5_glu_w2.astype(BF16)
    h_lat = _glu(z_lat.reshape(n_b * seq, d), glu_w1, glu_w2, 0, h_lat, mods, layer, 5,
                 tm=tm_mm, bidx=bidx_lat(tm_mm))
    h_ctx = _glu(z_ctx.reshape(n_b * l_len, d), glu_w1, glu_w2, 0, h_ctx, mods, layer, 5,
                 tm=tm_mm_ctx, bidx=bidx_ctx)

    h_lat = _ffn(h_lat, mods, layer, 6, norm_ffn2, *ffn2, tm=tm_lat, bidx=bidx_lat(tm_lat))
    h_ctx = _ffn(h_ctx, mods, layer, 6, norm_ffn2, *ffn2, tm=tm_ctx, bidx=bidx_ctx)

    layer = 1
    h_lat = _ffn(h_lat, mods, layer, 0, norm_ffn1, *ffn1, tm=tm_lat, bidx=bidx_lat(tm_lat))
    h_ctx = _ffn(h_ctx, mods, layer, 0, norm_ffn1, *ffn1, tm=tm_ctx, bidx=bidx_ctx)

    cos, sin = _rope_tables(seq)
    w_qkv = attn_w_qkv.astype(BF16)
    qkv_lat = _qkv(h_lat, mods, layer, 3, norm_mix, w_qkv, 0, attn_q_gain, attn_k_gain, cos, sin,
                   seq=seq, tm=tm_mm, bidx=bidx_lat(tm_mm), n_q_heads=n_q_heads, n_kv_heads=n_kv_heads,
                   kv_only=False, rope=True)
    no_rope = jnp.zeros((l_len, HEAD_DIM), F32)
    kv_ctx = _qkv(h_ctx, mods, layer, 3, norm_mix, w_qkv, 0, attn_q_gain, attn_k_gain, no_rope, no_rope,
                  seq=l_len, tm=tm_mm_ctx, bidx=bidx_ctx, n_q_heads=n_q_heads, n_kv_heads=n_kv_heads,
                  kv_only=True, rope=False)
    o_lat = _attention(qkv_lat, kv_ctx, n_q_heads, n_kv_heads, tq=min(256, seq))
    h_lat = _oproj(o_lat.reshape(n_b * seq, d), attn_w_o.astype(BF16), 0, h_lat, mods, layer, 5,
                   tm=tm_mm, bidx=bidx_lat(tm_mm))

    h_lat = _ffn(h_lat, mods, layer, 6, norm_ffn2, *ffn2, tm=tm_lat, bidx=bidx_lat(tm_lat))
    return h_lat.reshape(n_b, seq, d)
```

```python
import functools
import math

import jax
import jax.numpy as jnp
from jax import lax
from jax.experimental import pallas as pl
from jax.experimental.pallas import tpu as pltpu

F32 = jnp.float32
BF16 = jnp.bfloat16

EPS = 1e-6
N_MOD = 9
GRID_W = 64
HEAD_DIM = 128
KV_REP = 4
ROPE_THETA = 10000.0
S5_GROUP = 16
S5_STATE = 64

V7X_LANES = 128
V7X_SUBLANES = 8
V7X_VMEM_LIMIT = 60 * 1024 * 1024

S5_GB = 8
S5_TQ = 64
S5_PITCH = 68
COND_ROWS = 16
COL_BLK = 512
ATTN_ROWS = 256


def _cparams(sem):
    return pltpu.CompilerParams(dimension_semantics=sem, vmem_limit_bytes=V7X_VMEM_LIMIT)


def _resident(shape, index):
    return pl.BlockSpec(shape, index, pipeline_mode=pl.Buffered(1))


def _ada_kernel(a_ref, w_ref, b_ref, o_ref):
    a = a_ref[...]
    s = (a * jax.nn.sigmoid(a)).astype(BF16)
    o_ref[...] = jnp.dot(s, w_ref[...].astype(BF16), preferred_element_type=F32) + b_ref[...]


def _ada_mods(cond, ada_w, ada_b, tn=1024):
    depth, d, n = ada_w.shape
    rows = cond.shape[0]
    return pl.pallas_call(
        _ada_kernel,
        grid=(depth, n // tn),
        in_specs=[
            pl.BlockSpec((rows, d), lambda l, j: (0, 0)),
            pl.BlockSpec((None, d, tn), lambda l, j: (l, 0, j)),
            pl.BlockSpec((None, 1, tn), lambda l, j: (l, 0, j)),
        ],
        out_specs=pl.BlockSpec((None, rows, tn), lambda l, j: (l, 0, j)),
        out_shape=jax.ShapeDtypeStruct((depth, rows, n), F32),
        compiler_params=_cparams(("arbitrary", "arbitrary")),
        name="ada_mods",
    )(cond, ada_w, ada_b.reshape(depth, 1, n))


def _mod_spec(layer, col, d, bidx):
    return pl.BlockSpec((None, None, 1, d), lambda i, *_: (layer, bidx(i), 0, col))


def _norm_mod(x, g, shift, scale):
    r = lax.rsqrt(jnp.mean(x * x, axis=-1, keepdims=True) + EPS)
    return (x * r * g) * (1.0 + scale) + shift


def _ffn_kernel(h_ref, nrm_ref, sh_ref, sc_ref, gt_ref, wg_ref, wu_ref, wd_ref, o_ref, *rest, emit_w):
    m_ref = rest[-1]
    f = pl.program_id(1)

    @pl.when(f == 0)
    def _():
        x = h_ref[...]
        m_ref[...] = _norm_mod(x, nrm_ref[...], sh_ref[...], sc_ref[...]).astype(BF16)
        o_ref[...] = x

    m = m_ref[...]
    wg, wu = wg_ref[...].astype(BF16), wu_ref[...].astype(BF16)
    if emit_w:
        rest[0][...] = wg
        rest[1][...] = wu
    g = jnp.dot(m, wg, preferred_element_type=F32)
    u = jnp.dot(m, wu, preferred_element_type=F32)
    a = (g * jax.nn.sigmoid(g) * u).astype(BF16)
    for nb in range(o_ref.shape[1] // COL_BLK):
        cols = slice(nb * COL_BLK, (nb + 1) * COL_BLK)
        wd = wd_ref[:, cols].astype(BF16)
        if emit_w:
            rest[2][:, cols] = wd
        p = jnp.dot(a, wd, preferred_element_type=F32)
        o_ref[:, cols] += (0.5 * gt_ref[:, cols]) * p


def _ffn(h, mods, layer, mod_base, norm_g, w_gate, w_up, w_down, *, tm, bidx, emit_w=False):
    n_tok, d = h.shape
    f_dim = w_gate.shape[-1]
    if w_gate.ndim == 3:
        tf = 256
        wspec = lambda shape, idx: pl.BlockSpec((None,) + shape, lambda i, f: (layer,) + idx(f))
    else:
        assert not emit_w and w_gate.dtype == BF16
        tf = 512
        wspec = lambda shape, idx: pl.BlockSpec(shape, lambda i, f: idx(f))
    out_specs = [pl.BlockSpec((tm, d), lambda i, f: (i, 0))]
    out_shape = [jax.ShapeDtypeStruct((n_tok, d), F32)]
    if emit_w:
        assert n_tok == tm, "each weight tile must be visited exactly once"
        out_specs += [pl.BlockSpec((d, tf), lambda i, f: (0, f)), pl.BlockSpec((d, tf), lambda i, f: (0, f)),
                      pl.BlockSpec((tf, d), lambda i, f: (f, 0))]
        out_shape += [jax.ShapeDtypeStruct((d, f_dim), BF16), jax.ShapeDtypeStruct((d, f_dim), BF16),
                      jax.ShapeDtypeStruct((f_dim, d), BF16)]
    out = pl.pallas_call(
        functools.partial(_ffn_kernel, emit_w=emit_w),
        grid=(n_tok // tm, f_dim // tf),
        in_specs=[
            pl.BlockSpec((tm, d), lambda i, f: (i, 0), pipeline_mode=pl.Buffered(1)),
            pl.BlockSpec((None, 1, d), lambda i, f: (layer, 0, 0)),
            _mod_spec(layer, mod_base + 0, d, bidx),
            _mod_spec(layer, mod_base + 1, d, bidx),
            _mod_spec(layer, mod_base + 2, d, bidx),
            wspec((d, tf), lambda f: (0, f)),
            wspec((d, tf), lambda f: (0, f)),
            wspec((tf, d), lambda f: (f, 0)),
        ],
        out_specs=out_specs,
        out_shape=out_shape,
        scratch_shapes=[pltpu.VMEM((tm, d), BF16)],
        compiler_params=_cparams(("arbitrary", "arbitrary")),
        name="ffn",
    )(h, norm_g.reshape(norm_g.shape[0], 1, d), mods, mods, mods, w_gate, w_up, w_down)
    return (out[0], tuple(out[1:])) if emit_w else out[0]


def _zoh(a_re, a_im, log_dt):
    dt = jnp.exp(log_dt)
    mag = jnp.exp(a_re * dt)
    l_re = mag * jnp.cos(a_im * dt)
    l_im = mag * jnp.sin(a_im * dt)
    den = a_re * a_re + a_im * a_im
    c_re = ((l_re - 1.0) * a_re + l_im * a_im) / den
    c_im = (l_im * a_re - (l_re - 1.0) * a_im) / den
    return l_re, l_im, c_re, c_im


def _lane_tile(x, reps):
    w = x.shape[1]
    row = lax.broadcasted_iota(jnp.int32, (w, w * reps), 0)
    col = lax.broadcasted_iota(jnp.int32, (w, w * reps), 1)
    sel = (row == col % w).astype(BF16)
    return jnp.dot(x, sel, preferred_element_type=F32)


def _s5_param_kernel(are_ref, aim_ref, ldt_ref, bre_ref, bim_ref, cre_ref, cim_ref, ar2_ref, ai2_ref, ld2_ref,
                     obr_ref, obi_ref, ocr_ref, oci_ref, olr_ref, oli_ref):
    gpb = are_ref.shape[0] // S5_GROUP
    _, _, k_re, k_im = _zoh(are_ref[...], aim_ref[...], ldt_ref[...])
    bb_re = k_re * bre_ref[...] - k_im * bim_ref[...]
    bb_im = k_re * bim_ref[...] + k_im * bre_ref[...]
    n_in, n_st = bb_re.shape[0], S5_STATE * gpb
    on_diag = (lax.broadcasted_iota(jnp.int32, (n_in, n_st), 0) // S5_GROUP
               == lax.broadcasted_iota(jnp.int32, (n_in, n_st), 1) // S5_STATE)
    obr_ref[...] = jnp.where(on_diag, _lane_tile(bb_re.astype(BF16), gpb), 0.0).astype(BF16)
    obi_ref[...] = jnp.where(on_diag, _lane_tile(bb_im.astype(BF16), gpb), 0.0).astype(BF16)
    on_diag_t = (lax.broadcasted_iota(jnp.int32, (n_st, n_in), 0) // S5_STATE
                 == lax.broadcasted_iota(jnp.int32, (n_st, n_in), 1) // S5_GROUP)
    ocr_ref[...] = jnp.where(on_diag_t, _lane_tile(cre_ref[...].astype(BF16), gpb), 0.0).astype(BF16)
    oci_ref[...] = jnp.where(on_diag_t, _lane_tile((-cim_ref[...]).astype(BF16), gpb), 0.0).astype(BF16)
    l_re, l_im, _, _ = _zoh(ar2_ref[...], ai2_ref[...], ld2_ref[...])
    olr_ref[...] = l_re
    oli_ref[...] = l_im


def _s5_params(a_re, a_im, log_dt, b_re, b_im, c_re, c_im):
    dirs, g, p = a_re.shape
    hch = b_re.shape[3]
    gpb = g // S5_GB
    rows_in, rows_st = gpb * hch, gpb * p

    def per_channel(v):
        return jnp.broadcast_to(v[:, :, None, :], (dirs, g, hch, p)).reshape(dirs, S5_GB, rows_in, p)

    ldt = jnp.broadcast_to(log_dt[:, :, None], (dirs, g, p))
    args = (
        per_channel(a_re), per_channel(a_im), per_channel(ldt),
        jnp.swapaxes(b_re, 2, 3).reshape(dirs, S5_GB, rows_in, p),
        jnp.swapaxes(b_im, 2, 3).reshape(dirs, S5_GB, rows_in, p),
        jnp.swapaxes(c_re, 2, 3).reshape(dirs, S5_GB, rows_st, hch),
        jnp.swapaxes(c_im, 2, 3).reshape(dirs, S5_GB, rows_st, hch),
        a_re.reshape(dirs, g * p // V7X_LANES, V7X_LANES),
        a_im.reshape(dirs, g * p // V7X_LANES, V7X_LANES),
        ldt.reshape(dirs, g * p // V7X_LANES, V7X_LANES),
    )
    blk = lambda r, c: pl.BlockSpec((None, None, r, c), lambda dd, gb: (dd, gb, 0, 0))
    flat = pl.BlockSpec((None, g * p // V7X_LANES, V7X_LANES), lambda dd, gb: (dd, 0, 0))
    return pl.pallas_call(
        _s5_param_kernel,
        grid=(dirs, S5_GB),
        in_specs=[blk(rows_in, p)] * 5 + [blk(rows_st, hch)] * 2 + [flat] * 3,
        out_specs=[blk(rows_in, rows_st), blk(rows_in, rows_st), blk(rows_st, rows_in), blk(rows_st, rows_in),
                   flat, flat],
        out_shape=[
            jax.ShapeDtypeStruct((dirs, S5_GB, rows_in, rows_st), BF16),
            jax.ShapeDtypeStruct((dirs, S5_GB, rows_in, rows_st), BF16),
            jax.ShapeDtypeStruct((dirs, S5_GB, rows_st, rows_in), BF16),
            jax.ShapeDtypeStruct((dirs, S5_GB, rows_st, rows_in), BF16),
            jax.ShapeDtypeStruct((dirs, g * p // V7X_LANES, V7X_LANES), F32),
            jax.ShapeDtypeStruct((dirs, g * p // V7X_LANES, V7X_LANES), F32),
        ],
        compiler_params=_cparams(("arbitrary", "arbitrary")),
        name="s5_params",
    )(*args)


def _pair_rows(lam):
    dirs, tiles, lanes = lam.shape
    half = V7X_SUBLANES // 2
    x = jnp.broadcast_to(lam.reshape(dirs, 2, tiles // 2, 1, lanes), (dirs, 2, tiles // 2, half, lanes))
    return jnp.transpose(x, (0, 2, 1, 3, 4)).reshape(dirs, tiles // 2, V7X_SUBLANES, lanes)


def _s5_scan_kernel(*refs, n_ctx_chunks, tq, pitch, reverse, ctx_row, emit_z):
    gb_per_half = S5_GB // 2
    n_in = 16 if emit_z else 13
    (hl_ref, hc_ref, nrm_ref, sh_ref, sc_ref, bre_ref, bim_ref, cre_ref, cim_ref, are_ref, aim_ref) = refs[:11]
    if emit_z:
        yl_ref, yc_ref, dsk_ref = refs[11:14]
    ol_ref, oc_ref = refs[n_in - 2:n_in]
    u_st = refs[n_in]
    s_re_k = refs[n_in + 1:n_in + 1 + gb_per_half]
    s_im_k = refs[n_in + 1 + gb_per_half:n_in + 1 + 2 * gb_per_half]
    h_re, h_im = refs[n_in + 1 + 2 * gb_per_half:]
    c = pl.program_id(0)
    n_b = hl_ref.shape[0]
    half_rows = n_b * pitch
    tiles_per_gb = s_re_k[0].shape[0]
    lanes_in = u_st.shape[1] // S5_GB

    @pl.when(c == 0)
    def _():
        h_re[...] = jnp.zeros_like(h_re)
        h_im[...] = jnp.zeros_like(h_im)
        u_st[...] = jnp.zeros_like(u_st)

    is_ctx = c < n_ctx_chunks

    @pl.when(is_ctx)
    def _():
        for b in range(n_b):
            u_st[b * pitch:b * pitch + tq, :] = _norm_mod(hc_ref[b], nrm_ref[...], sh_ref[ctx_row], sc_ref[ctx_row])

    @pl.when(jnp.logical_not(is_ctx))
    def _():
        for b in range(n_b):
            u_st[b * pitch:b * pitch + tq, :] = _norm_mod(hl_ref[b], nrm_ref[...], sh_ref[b], sc_ref[b])

    def project_in(gb):
        lhs = u_st[:, gb * lanes_in:(gb + 1) * lanes_in].astype(BF16)
        p_re = jnp.dot(lhs, bre_ref[gb], preferred_element_type=F32)
        p_im = jnp.dot(lhs, bim_ref[gb], preferred_element_type=F32)
        half = gb // gb_per_half
        rows = slice(half * half_rows, (half + 1) * half_rows)
        s_re, s_im = s_re_k[gb % gb_per_half], s_im_k[gb % gb_per_half]
        for j in range(tiles_per_gb):
            s_re[j, rows, :] = p_re[:, j * V7X_LANES:(j + 1) * V7X_LANES]
            s_im[j, rows, :] = p_im[:, j * V7X_LANES:(j + 1) * V7X_LANES]

    def scan_pass(k):
        s_re, s_im = s_re_k[k], s_im_k[k]
        ns = list(range(k * tiles_per_gb, (k + 1) * tiles_per_gb))
        a_r = [are_ref[n] for n in ns]
        a_i = [aim_ref[n] for n in ns]
        hr = [h_re[n] for n in ns]
        hi = [h_im[n] for n in ns]
        for t in range(tq):
            rows = pl.ds(tq - 1 - t if reverse else t, V7X_SUBLANES, stride=pitch)
            for j in range(tiles_per_gb):
                n_r = a_r[j] * hr[j] - a_i[j] * hi[j] + s_re[j, rows, :]
                n_i = a_r[j] * hi[j] + a_i[j] * hr[j] + s_im[j, rows, :]
                s_re[j, rows, :] = n_r
                s_im[j, rows, :] = n_i
                hr[j], hi[j] = n_r, n_i
        for j, n in enumerate(ns):
            h_re[n] = hr[j]
            h_im[n] = hi[j]

    def project_out(gb):
        half = gb // gb_per_half
        rows = slice(half * half_rows, (half + 1) * half_rows)
        s_re, s_im = s_re_k[gb % gb_per_half], s_im_k[gb % gb_per_half]
        l_re = jnp.concatenate([s_re[j, rows, :] for j in range(tiles_per_gb)], axis=1).astype(BF16)
        l_im = jnp.concatenate([s_im[j, rows, :] for j in range(tiles_per_gb)], axis=1).astype(BF16)
        return (jnp.dot(l_re, cre_ref[gb], preferred_element_type=F32)
                + jnp.dot(l_im, cim_ref[gb], preferred_element_type=F32))

    for gb in range(S5_GB):
        project_in(gb)
    for k in range(gb_per_half):
        scan_pass(k)
    ys = [project_out(gb) for gb in range(S5_GB)]

    def emit(out_ref, yf_ref):
        for gb in range(S5_GB):
            cols = slice(gb * lanes_in, (gb + 1) * lanes_in)
            for b in range(n_b):
                y = ys[gb][b * pitch:b * pitch + tq]
                if emit_z:
                    y = y + yf_ref[b, :, cols] + dsk_ref[:, cols] * u_st[b * pitch:b * pitch + tq, cols]
                    out_ref[b, :, cols] = _gelu_tanh(y).astype(out_ref.dtype)
                else:
                    out_ref[b, :, cols] = y

    @pl.when(is_ctx)
    def _():
        emit(oc_ref, yc_ref if emit_z else None)

    @pl.when(jnp.logical_not(is_ctx))
    def _():
        emit(ol_ref, yl_ref if emit_z else None)


def _gelu_tanh(x):
    return 0.5 * x * (1.0 + jnp.tanh(math.sqrt(2.0 / math.pi) * (x + 0.044715 * (x * x * x))))


def _s5_scan(h_lat, h_ctx, mods, layer, mod_base, norm_g, maps, direction, y_prev=None, d_skip=None):
    n_b, s_len, d = h_lat.shape
    l_len = h_ctx.shape[1]
    tq, pitch = S5_TQ, S5_PITCH
    n_l, n_s = l_len // tq, s_len // tq
    bd_bre, bd_bim, bd_cre, bd_cim, a_re, a_im = maps
    n_pairs = a_re.shape[1]
    reverse = direction == 1
    emit_z = y_prev is not None

    def lat_idx(c):
        k = jnp.maximum(c - n_l, 0)
        return n_s - 1 - k if reverse else k

    def ctx_idx(c):
        k = jnp.minimum(c, n_l - 1)
        return n_l - 1 - k if reverse else k

    def wspec(arr):
        shape = arr.shape[1:]
        return _resident((None,) + shape, lambda c: (direction,) + (0,) * len(shape))

    lat_spec = pl.BlockSpec((n_b, tq, d), lambda c: (0, lat_idx(c), 0))
    ctx_spec = pl.BlockSpec((n_b, tq, d), lambda c: (0, ctx_idx(c), 0))
    mod_rows = lambda col: pl.BlockSpec((None, COND_ROWS, 1, d), lambda c: (layer, 0, 0, col))
    in_specs = [lat_spec, ctx_spec,
                pl.BlockSpec((None, 1, d), lambda c: (layer, 0, 0)),
                mod_rows(mod_base), mod_rows(mod_base + 1),
                wspec(bd_bre), wspec(bd_bim), wspec(bd_cre), wspec(bd_cim), wspec(a_re), wspec(a_im)]
    args = [h_lat, h_ctx, norm_g.reshape(norm_g.shape[0], 1, d), mods, mods,
            bd_bre, bd_bim, bd_cre, bd_cim, a_re, a_im]
    out_dtype = F32
    if emit_z:
        in_specs += [lat_spec, ctx_spec, pl.BlockSpec((None, 1, d), lambda c: (0, 0, 0))]
        args += [y_prev[0], y_prev[1], d_skip.reshape(d_skip.shape[0], 1, d)]
        out_dtype = BF16
    kern = functools.partial(_s5_scan_kernel, n_ctx_chunks=n_l, tq=tq, pitch=pitch, reverse=reverse,
                             ctx_row=n_b, emit_z=emit_z)
    return pl.pallas_call(
        kern,
        grid=(n_l + n_s,),
        in_specs=in_specs,
        out_specs=[lat_spec, ctx_spec],
        out_shape=[jax.ShapeDtypeStruct((n_b, s_len, d), out_dtype),
                   jax.ShapeDtypeStruct((n_b, l_len, d), out_dtype)],
        scratch_shapes=[
            pltpu.VMEM((n_b * pitch, d), F32),
            *[pltpu.VMEM((2 * n_pairs // S5_GB, 2 * n_b * pitch, V7X_LANES), F32) for _ in range(S5_GB)],
            pltpu.VMEM((n_pairs, V7X_SUBLANES, V7X_LANES), F32),
            pltpu.VMEM((n_pairs, V7X_SUBLANES, V7X_LANES), F32),
        ],
        compiler_params=_cparams(("arbitrary",)),
        name="s5_bwd" if reverse else "s5_fwd",
    )(*args)


def _glu_kernel(z_ref, w1_ref, w2_ref, h_ref, gt_ref, o_ref):
    z = z_ref[...]
    for nb in range(o_ref.shape[1] // COL_BLK):
        cols = slice(nb * COL_BLK, (nb + 1) * COL_BLK)
        a = jnp.dot(z, w1_ref[:, cols], preferred_element_type=F32)
        b = jnp.dot(z, w2_ref[:, cols], preferred_element_type=F32)
        o_ref[:, cols] = h_ref[:, cols] + gt_ref[:, cols] * (a * jax.nn.sigmoid(b))


def _glu(z, w1, w2, sub, h, mods, layer, gate_col, *, tm, bidx):
    n_tok, d = h.shape
    return pl.pallas_call(
        _glu_kernel,
        grid=(n_tok // tm,),
        in_specs=[
            pl.BlockSpec((tm, d), lambda i: (i, 0)),
            _resident((None, d, d), lambda i: (sub, 0, 0)),
            _resident((None, d, d), lambda i: (sub, 0, 0)),
            pl.BlockSpec((tm, d), lambda i: (i, 0)),
            _mod_spec(layer, gate_col, d, bidx),
        ],
        out_specs=pl.BlockSpec((tm, d), lambda i: (i, 0)),
        out_shape=jax.ShapeDtypeStruct((n_tok, d), F32),
        compiler_params=_cparams(("arbitrary",)),
        name="s5_glu",
    )(z, w1, w2, h, mods)


def _rope_tables(seq):
    pairs = HEAD_DIM // 4
    freqs = ROPE_THETA ** (-jnp.arange(pairs, dtype=F32) / pairs)
    pos = jnp.arange(seq, dtype=jnp.int32)
    ang_r = (pos // GRID_W).astype(F32)[:, None] * freqs
    ang_c = (pos % GRID_W).astype(F32)[:, None] * freqs
    cos = jnp.concatenate([jnp.cos(ang_r), jnp.cos(ang_r), jnp.cos(ang_c), jnp.cos(ang_c)], axis=-1)
    sin = jnp.concatenate([-jnp.sin(ang_r), jnp.sin(ang_r), -jnp.sin(ang_c), jnp.sin(ang_c)], axis=-1)
    return cos, sin


def _qkv_kernel(h_ref, nrm_ref, sh_ref, sc_ref, w_ref, qg_ref, kg_ref, cos_ref, sin_ref, o_ref,
                *, head0, n_q_heads, n_kv_heads, rope):
    m = _norm_mod(h_ref[...], nrm_ref[...], sh_ref[...], sc_ref[...]).astype(BF16)
    lane = lax.broadcasted_iota(jnp.int32, (1, HEAD_DIM), 1)
    first = (lane % (HEAD_DIM // 2)) < (HEAD_DIM // 4)
    hpb = COL_BLK // HEAD_DIM
    for nb in range(w_ref.shape[1] // COL_BLK):
        acc = jnp.dot(m, w_ref[:, nb * COL_BLK:(nb + 1) * COL_BLK], preferred_element_type=F32)
        for hh in range(hpb):
            head = head0 + nb * hpb + hh
            x = acc[:, hh * HEAD_DIM:(hh + 1) * HEAD_DIM]
            if head < n_q_heads + n_kv_heads:
                gain = qg_ref[...] if head < n_q_heads else kg_ref[...]
                xn = x * lax.rsqrt(jnp.mean(x * x, axis=-1, keepdims=True) + EPS) * gain
                if rope:
                    partner = jnp.where(first, pltpu.roll(xn, HEAD_DIM - HEAD_DIM // 4, 1),
                                        pltpu.roll(xn, HEAD_DIM // 4, 1))
                    xn = xn * cos_ref[...] + partner * sin_ref[...]
                x = xn
            o_ref[nb * hpb + hh] = x.astype(BF16)


def _qkv(h, mods, layer, mod_base, norm_g, w_qkv, sub, q_gain, k_gain, cos, sin, *, seq, tm, bidx,
         n_q_heads, n_kv_heads, kv_only, rope):
    n_tok, d = h.shape
    n_b = n_tok // seq
    n_cols = w_qkv.shape[2]
    head0 = n_q_heads if kv_only else 0
    width = n_cols - head0 * HEAD_DIM
    heads = width // HEAD_DIM
    assert (head0 * HEAD_DIM) % width == 0
    tiles_per_seq = seq // tm
    kern = functools.partial(_qkv_kernel, head0=head0, n_q_heads=n_q_heads, n_kv_heads=n_kv_heads, rope=rope)
    return pl.pallas_call(
        kern,
        grid=(n_tok // tm,),
        in_specs=[
            pl.BlockSpec((tm, d), lambda i: (i, 0)),
            pl.BlockSpec((None, 1, d), lambda i: (layer, 0, 0)),
            _mod_spec(layer, mod_base + 0, d, bidx),
            _mod_spec(layer, mod_base + 1, d, bidx),
            _resident((None, d, width), lambda i: (sub, 0, head0 * HEAD_DIM // width)),
            pl.BlockSpec((None, 1, HEAD_DIM), lambda i: (sub, 0, 0)),
            pl.BlockSpec((None, 1, HEAD_DIM), lambda i: (sub, 0, 0)),
            pl.BlockSpec((tm, HEAD_DIM), lambda i: (i % tiles_per_seq, 0)),
            pl.BlockSpec((tm, HEAD_DIM), lambda i: (i % tiles_per_seq, 0)),
        ],
        out_specs=pl.BlockSpec((None, heads, tm, HEAD_DIM), lambda i: (i // tiles_per_seq, 0, i % tiles_per_seq, 0)),
        out_shape=jax.ShapeDtypeStruct((n_b, heads, seq, HEAD_DIM), BF16),
        compiler_params=_cparams(("arbitrary",)),
        name="qkv",
    )(h, norm_g.reshape(norm_g.shape[0], 1, d), mods, mods, w_qkv,
      q_gain.reshape(q_gain.shape[0], 1, HEAD_DIM), k_gain.reshape(k_gain.shape[0], 1, HEAD_DIM), cos, sin)


def _attn_kernel(q_ref, kl_ref, vl_ref, kc_ref, vc_ref, o_ref):
    rep, tq, hd = q_ref.shape
    dn = (((1,), (1,)), ((), ()))
    c = (HEAD_DIM ** -0.5) * math.log2(math.e)
    n_sub = tq // ATTN_ROWS
    for r in range(rep):
        for sb in range(n_sub):
            rows = slice(sb * ATTN_ROWS, (sb + 1) * ATTN_ROWS)
            q = q_ref[r, rows, :]
            s_l = lax.dot_general(q, kl_ref[...], dn, preferred_element_type=F32)
            s_c = lax.dot_general(q, kc_ref[...], dn, preferred_element_type=F32)
            m = jnp.maximum(jnp.max(s_l, axis=-1, keepdims=True), jnp.max(s_c, axis=-1, keepdims=True))
            p_l = jnp.exp2((s_l - m) * c)
            p_c = jnp.exp2((s_c - m) * c)
            den = jnp.sum(p_l, axis=-1, keepdims=True) + jnp.sum(p_c, axis=-1, keepdims=True)
            o = (jnp.dot(p_l.astype(BF16), vl_ref[...], preferred_element_type=F32)
                 + jnp.dot(p_c.astype(BF16), vc_ref[...], preferred_element_type=F32)) / den
            o_ref[rows, r * hd:(r + 1) * hd] = o.astype(BF16)


def _attention(qkv_lat, kv_ctx, n_q_heads, n_kv_heads, *, tq=256):
    n_b, _, seq, hd = qkv_lat.shape
    l_len = kv_ctx.shape[2]
    rep = n_q_heads // n_kv_heads
    return pl.pallas_call(
        _attn_kernel,
        grid=(n_b, n_kv_heads, seq // tq),
        in_specs=[
            pl.BlockSpec((None, rep, tq, hd), lambda b, g, i: (b, g, i, 0)),
            pl.BlockSpec((None, None, seq, hd), lambda b, g, i: (b, n_q_heads + g, 0, 0)),
            pl.BlockSpec((None, None, seq, hd), lambda b, g, i: (b, n_q_heads + n_kv_heads + g, 0, 0)),
            pl.BlockSpec((None, None, l_len, hd), lambda b, g, i: (b, g, 0, 0)),
            pl.BlockSpec((None, None, l_len, hd), lambda b, g, i: (b, n_kv_heads + g, 0, 0)),
        ],
        out_specs=pl.BlockSpec((None, tq, rep * hd), lambda b, g, i: (b, i, g)),
        out_shape=jax.ShapeDtypeStruct((n_b, seq, n_q_heads * hd), BF16),
        compiler_params=_cparams(("arbitrary", "arbitrary", "arbitrary")),
        name="attention",
    )(qkv_lat, qkv_lat, qkv_lat, kv_ctx, kv_ctx)


def _oproj_kernel(x_ref, w_ref, h_ref, gt_ref, o_ref):
    x = x_ref[...]
    for nb in range(o_ref.shape[1] // COL_BLK):
        cols = slice(nb * COL_BLK, (nb + 1) * COL_BLK)
        acc = jnp.dot(x, w_ref[:, cols], preferred_element_type=F32)
        o_ref[:, cols] = h_ref[:, cols] + gt_ref[:, cols] * acc


def _oproj(x, w_o, sub, h, mods, layer, gate_col, *, tm, bidx):
    n_tok, d = h.shape
    k_dim = x.shape[1]
    return pl.pallas_call(
        _oproj_kernel,
        grid=(n_tok // tm,),
        in_specs=[
            pl.BlockSpec((tm, k_dim), lambda i: (i, 0)),
            _resident((None, k_dim, d), lambda i: (sub, 0, 0)),
            pl.BlockSpec((tm, d), lambda i: (i, 0)),
            _mod_spec(layer, gate_col, d, bidx),
        ],
        out_specs=pl.BlockSpec((tm, d), lambda i: (i, 0)),
        out_shape=jax.ShapeDtypeStruct((n_tok, d), F32),
        compiler_params=_cparams(("arbitrary",)),
        name="oproj",
    )(x, w_o, h, mods)


def kernel(x, c, ctx, c_ctx, ada_w, ada_b, norm_ffn1, norm_mix, norm_ffn2, ffn1_w_gate, ffn1_w_up, ffn1_w_down, ffn2_w_gate, ffn2_w_up, ffn2_w_down, s5_a_re, s5_a_im, s5_log_dt, s5_b_re, s5_b_im, s5_c_re, s5_c_im, s5_d, s5_glu_w1, s5_glu_w2, attn_w_qkv, attn_w_o, attn_q_gain, attn_k_gain):
    n_b, seq, d = x.shape
    l_len = ctx.shape[1]
    depth = ada_w.shape[0]
    assert depth == 2 and ada_w.shape[2] == N_MOD * d
    assert n_b == 4, "the S5 scan packs 4 sequences x 2 lane tiles onto the 8 sublanes"
    n_q_heads = d // HEAD_DIM
    n_kv_heads = n_q_heads // KV_REP

    tm_lat = min(1024, seq)
    tm_ctx = min(1024, n_b * l_len)
    tm_mm = min(512, seq)
    tm_mm_ctx = min(512, l_len)
    ctx_row = n_b
    bidx_lat = lambda tm: (lambda i: (i * tm) // seq)
    bidx_ctx = lambda i: ctx_row

    cond = jnp.concatenate([c, c_ctx[None, :], jnp.zeros((COND_ROWS - n_b - 1, d), F32)], axis=0)
    mods = _ada_mods(cond, ada_w, ada_b).reshape(depth, COND_ROWS, 1, N_MOD * d)

    ffn1 = (ffn1_w_gate, ffn1_w_up, ffn1_w_down)
    ffn2 = (ffn2_w_gate, ffn2_w_up, ffn2_w_down)

    h_lat = x.reshape(n_b * seq, d)
    h_ctx = ctx.reshape(n_b * l_len, d)

    def ffn_pair(h_lat, h_ctx, layer, mod_base, norm_g, weights):
        h_ctx, w_bf16 = _ffn(h_ctx, mods, layer, mod_base, norm_g, *weights, tm=tm_ctx, bidx=bidx_ctx, emit_w=True)
        h_lat = _ffn(h_lat, mods, layer, mod_base, norm_g, *w_bf16, tm=tm_lat, bidx=bidx_lat(tm_lat))
        return h_lat, h_ctx

    layer = 0
    h_lat, h_ctx = ffn_pair(h_lat, h_ctx, layer, 0, norm_ffn1, ffn1)

    params = _s5_params(s5_a_re[0], s5_a_im[0], s5_log_dt[0], s5_b_re[0], s5_b_im[0], s5_c_re[0], s5_c_im[0])
    maps = tuple(params[:4]) + (_pair_rows(params[4]), _pair_rows(params[5]))
    h3_lat, h3_ctx = h_lat.reshape(n_b, seq, d), h_ctx.reshape(n_b, l_len, d)
    y_fwd = _s5_scan(h3_lat, h3_ctx, mods, layer, 3, norm_mix, maps, 0)
    z_lat, z_ctx = _s5_scan(h3_lat, h3_ctx, mods, layer, 3, norm_mix, maps, 1, y_prev=y_fwd, d_skip=s5_d)
    glu_w1, glu_w2 = s5_glu_w1.astype(BF16), s5_glu_w2.astype(BF16)
    h_lat = _glu(z_lat.reshape(n_b * seq, d), glu_w1, glu_w2, 0, h_lat, mods, layer, 5,
                 tm=tm_mm, bidx=bidx_lat(tm_mm))
    h_ctx = _glu(z_ctx.reshape(n_b * l_len, d), glu_w1, glu_w2, 0, h_ctx, mods, layer, 5,
                 tm=tm_mm_ctx, bidx=bidx_ctx)

    h_lat, h_ctx = ffn_pair(h_lat, h_ctx, layer, 6, norm_ffn2, ffn2)

    layer = 1
    h_lat, h_ctx = ffn_pair(h_lat, h_ctx, layer, 0, norm_ffn1, ffn1)

    cos, sin = _rope_tables(seq)
    w_qkv = attn_w_qkv.astype(BF16)
    qkv_lat = _qkv(h_lat, mods, layer, 3, norm_mix, w_qkv, 0, attn_q_gain, attn_k_gain, cos, sin,
                   seq=seq, tm=tm_mm, bidx=bidx_lat(tm_mm), n_q_heads=n_q_heads, n_kv_heads=n_kv_heads,
                   kv_only=False, rope=True)
    no_rope = jnp.zeros((l_len, HEAD_DIM), F32)
    kv_ctx = _qkv(h_ctx, mods, layer, 3, norm_mix, w_qkv, 0, attn_q_gain, attn_k_gain, no_rope, no_rope,
                  seq=l_len, tm=tm_mm_ctx, bidx=bidx_ctx, n_q_heads=n_q_heads, n_kv_heads=n_kv_heads,
                  kv_only=True, rope=False)
    o_lat = _attention(qkv_lat, kv_ctx, n_q_heads, n_kv_heads, tq=min(512, seq))
    h_lat = _oproj(o_lat.reshape(n_b * seq, d), attn_w_o.astype(BF16), 0, h_lat, mods, layer, 5,
                   tm=tm_mm, bidx=bidx_lat(tm_mm))

    h_lat = _ffn(h_lat, mods, layer, 6, norm_ffn2, *ffn2, tm=tm_lat, bidx=bidx_lat(tm_lat))
    return h_lat.reshape(n_b, seq, d)
```

```python
import functools
import math

import jax
import jax.numpy as jnp
from jax import lax
from jax.experimental import pallas as pl
from jax.experimental.pallas import tpu as pltpu

F32 = jnp.float32
BF16 = jnp.bfloat16

EPS = 1e-6
N_MOD = 9
GRID_W = 64
HEAD_DIM = 128
KV_REP = 4
ROPE_THETA = 10000.0
S5_GROUP = 16
S5_STATE = 64

V7X_LANES = 128
V7X_SUBLANES = 8
V7X_VMEM_LIMIT = 60 * 1024 * 1024

S5_GB = 8
S5_TQ = 64
S5_PITCH = 68
COND_ROWS = 16
COL_BLK = 512
NORM_ROWS = 32
ATTN_AHEAD = 1
ATTN_ROWS = 256


def _cparams(sem):
    return pltpu.CompilerParams(dimension_semantics=sem, vmem_limit_bytes=V7X_VMEM_LIMIT)


def _resident(shape, index):
    return pl.BlockSpec(shape, index, pipeline_mode=pl.Buffered(1))


def _ada_kernel(a_ref, w_ref, b_ref, o_ref):
    a = a_ref[...]
    s = (a * jax.nn.sigmoid(a)).astype(BF16)
    o_ref[...] = jnp.dot(s, w_ref[...].astype(BF16), preferred_element_type=F32) + b_ref[...]


def _ada_mods(cond, ada_w, ada_b, tn=1024):
    depth, d, n = ada_w.shape
    rows = cond.shape[0]
    return pl.pallas_call(
        _ada_kernel,
        grid=(depth, n // tn),
        in_specs=[
            pl.BlockSpec((rows, d), lambda l, j: (0, 0)),
            pl.BlockSpec((None, d, tn), lambda l, j: (l, 0, j)),
            pl.BlockSpec((None, 1, tn), lambda l, j: (l, 0, j)),
        ],
        out_specs=pl.BlockSpec((None, rows, tn), lambda l, j: (l, 0, j)),
        out_shape=jax.ShapeDtypeStruct((depth, rows, n), F32),
        compiler_params=_cparams(("arbitrary", "arbitrary")),
        name="ada_mods",
    )(cond, ada_w, ada_b.reshape(depth, 1, n))


def _mod_spec(layer, col, d, bidx):
    return pl.BlockSpec((None, None, 1, d), lambda i, *_: (layer, bidx(i), 0, col))


def _norm_mod(x, g, shift, scale):
    r = lax.rsqrt(jnp.mean(x * x, axis=-1, keepdims=True) + EPS)
    return (x * r * g) * (1.0 + scale) + shift


def _norm_mod_blocks(h_ref, nrm_ref, sh_ref, sc_ref, emit):
    g, shift, scale = nrm_ref[...], sh_ref[...], sc_ref[...]

    def body(i, carry):
        rows = pl.ds(pl.multiple_of(i * NORM_ROWS, NORM_ROWS), NORM_ROWS)
        x = h_ref[rows, :]
        emit(rows, x, _norm_mod(x, g, shift, scale))
        return carry

    lax.fori_loop(0, h_ref.shape[0] // NORM_ROWS, body, 0, unroll=4)


def _ffn_kernel(h_ref, nrm_ref, sh_ref, sc_ref, gt_ref, wg_ref, wu_ref, wd_ref, o_ref, *rest, emit_w):
    m_ref = rest[-1]
    f = pl.program_id(1)

    @pl.when(f == 0)
    def _():
        def emit(rows, x, m):
            m_ref[rows, :] = m.astype(BF16)
            o_ref[rows, :] = x

        _norm_mod_blocks(h_ref, nrm_ref, sh_ref, sc_ref, emit)

    m = m_ref[...]
    wg, wu = wg_ref[...].astype(BF16), wu_ref[...].astype(BF16)
    if emit_w:
        rest[0][...] = wg
        rest[1][...] = wu
    g = jnp.dot(m, wg, preferred_element_type=F32)
    u = jnp.dot(m, wu, preferred_element_type=F32)
    a = (g * jax.nn.sigmoid(g) * u).astype(BF16)
    for nb in range(o_ref.shape[1] // COL_BLK):
        cols = slice(nb * COL_BLK, (nb + 1) * COL_BLK)
        wd = wd_ref[:, cols].astype(BF16)
        if emit_w:
            rest[2][:, cols] = wd
        p = jnp.dot(a, wd, preferred_element_type=F32)
        o_ref[:, cols] += (0.5 * gt_ref[:, cols]) * p


def _ffn(h, mods, layer, mod_base, norm_g, w_gate, w_up, w_down, *, tm, bidx, emit_w=False):
    n_tok, d = h.shape
    f_dim = w_gate.shape[-1]
    if w_gate.ndim == 3:
        tf = 256
        wspec = lambda shape, idx: pl.BlockSpec((None,) + shape, lambda i, f: (layer,) + idx(f))
    else:
        assert not emit_w and w_gate.dtype == BF16
        tf = 512
        wspec = lambda shape, idx: pl.BlockSpec(shape, lambda i, f: idx(f))
    out_specs = [pl.BlockSpec((tm, d), lambda i, f: (i, 0))]
    out_shape = [jax.ShapeDtypeStruct((n_tok, d), F32)]
    if emit_w:
        assert n_tok == tm, "each weight tile must be visited exactly once"
        out_specs += [pl.BlockSpec((d, tf), lambda i, f: (0, f)), pl.BlockSpec((d, tf), lambda i, f: (0, f)),
                      pl.BlockSpec((tf, d), lambda i, f: (f, 0))]
        out_shape += [jax.ShapeDtypeStruct((d, f_dim), BF16), jax.ShapeDtypeStruct((d, f_dim), BF16),
                      jax.ShapeDtypeStruct((f_dim, d), BF16)]
    out = pl.pallas_call(
        functools.partial(_ffn_kernel, emit_w=emit_w),
        grid=(n_tok // tm, f_dim // tf),
        in_specs=[
            pl.BlockSpec((tm, d), lambda i, f: (i, 0), pipeline_mode=pl.Buffered(1)),
            pl.BlockSpec((None, 1, d), lambda i, f: (layer, 0, 0)),
            _mod_spec(layer, mod_base + 0, d, bidx),
            _mod_spec(layer, mod_base + 1, d, bidx),
            _mod_spec(layer, mod_base + 2, d, bidx),
            wspec((d, tf), lambda f: (0, f)),
            wspec((d, tf), lambda f: (0, f)),
            wspec((tf, d), lambda f: (f, 0)),
        ],
        out_specs=out_specs,
        out_shape=out_shape,
        scratch_shapes=[pltpu.VMEM((tm, d), BF16)],
        compiler_params=_cparams(("arbitrary", "arbitrary")),
        name="ffn",
    )(h, norm_g.reshape(norm_g.shape[0], 1, d), mods, mods, mods, w_gate, w_up, w_down)
    return (out[0], tuple(out[1:])) if emit_w else out[0]


def _zoh(a_re, a_im, log_dt):
    dt = jnp.exp(log_dt)
    mag = jnp.exp(a_re * dt)
    l_re = mag * jnp.cos(a_im * dt)
    l_im = mag * jnp.sin(a_im * dt)
    den = a_re * a_re + a_im * a_im
    c_re = ((l_re - 1.0) * a_re + l_im * a_im) / den
    c_im = (l_im * a_re - (l_re - 1.0) * a_im) / den
    return l_re, l_im, c_re, c_im


def _lane_tile(x, reps):
    w = x.shape[1]
    row = lax.broadcasted_iota(jnp.int32, (w, w * reps), 0)
    col = lax.broadcasted_iota(jnp.int32, (w, w * reps), 1)
    sel = (row == col % w).astype(BF16)
    return jnp.dot(x, sel, preferred_element_type=F32)


def _s5_param_kernel(are_ref, aim_ref, ldt_ref, bre_ref, bim_ref, cre_ref, cim_ref, ar2_ref, ai2_ref, ld2_ref,
                     obr_ref, obi_ref, ocr_ref, oci_ref, olr_ref, oli_ref):
    gpb = are_ref.shape[0] // S5_GROUP
    _, _, k_re, k_im = _zoh(are_ref[...], aim_ref[...], ldt_ref[...])
    bb_re = k_re * bre_ref[...] - k_im * bim_ref[...]
    bb_im = k_re * bim_ref[...] + k_im * bre_ref[...]
    n_in, n_st = bb_re.shape[0], S5_STATE * gpb
    on_diag = (lax.broadcasted_iota(jnp.int32, (n_in, n_st), 0) // S5_GROUP
               == lax.broadcasted_iota(jnp.int32, (n_in, n_st), 1) // S5_STATE)
    obr_ref[...] = jnp.where(on_diag, _lane_tile(bb_re.astype(BF16), gpb), 0.0).astype(BF16)
    obi_ref[...] = jnp.where(on_diag, _lane_tile(bb_im.astype(BF16), gpb), 0.0).astype(BF16)
    on_diag_t = (lax.broadcasted_iota(jnp.int32, (n_st, n_in), 0) // S5_STATE
                 == lax.broadcasted_iota(jnp.int32, (n_st, n_in), 1) // S5_GROUP)
    ocr_ref[...] = jnp.where(on_diag_t, _lane_tile(cre_ref[...].astype(BF16), gpb), 0.0).astype(BF16)
    oci_ref[...] = jnp.where(on_diag_t, _lane_tile((-cim_ref[...]).astype(BF16), gpb), 0.0).astype(BF16)
    l_re, l_im, _, _ = _zoh(ar2_ref[...], ai2_ref[...], ld2_ref[...])
    olr_ref[...] = l_re
    oli_ref[...] = l_im


def _s5_params(a_re, a_im, log_dt, b_re, b_im, c_re, c_im):
    dirs, g, p = a_re.shape
    hch = b_re.shape[3]
    gpb = g // S5_GB
    rows_in, rows_st = gpb * hch, gpb * p

    def per_channel(v):
        return jnp.broadcast_to(v[:, :, None, :], (dirs, g, hch, p)).reshape(dirs, S5_GB, rows_in, p)

    ldt = jnp.broadcast_to(log_dt[:, :, None], (dirs, g, p))
    args = (
        per_channel(a_re), per_channel(a_im), per_channel(ldt),
        jnp.swapaxes(b_re, 2, 3).reshape(dirs, S5_GB, rows_in, p),
        jnp.swapaxes(b_im, 2, 3).reshape(dirs, S5_GB, rows_in, p),
        jnp.swapaxes(c_re, 2, 3).reshape(dirs, S5_GB, rows_st, hch),
        jnp.swapaxes(c_im, 2, 3).reshape(dirs, S5_GB, rows_st, hch),
        a_re.reshape(dirs, g * p // V7X_LANES, V7X_LANES),
        a_im.reshape(dirs, g * p // V7X_LANES, V7X_LANES),
        ldt.reshape(dirs, g * p // V7X_LANES, V7X_LANES),
    )
    blk = lambda r, c: pl.BlockSpec((None, None, r, c), lambda dd, gb: (dd, gb, 0, 0))
    flat = pl.BlockSpec((None, g * p // V7X_LANES, V7X_LANES), lambda dd, gb: (dd, 0, 0))
    return pl.pallas_call(
        _s5_param_kernel,
        grid=(dirs, S5_GB),
        in_specs=[blk(rows_in, p)] * 5 + [blk(rows_st, hch)] * 2 + [flat] * 3,
        out_specs=[blk(rows_in, rows_st), blk(rows_in, rows_st), blk(rows_st, rows_in), blk(rows_st, rows_in),
                   flat, flat],
        out_shape=[
            jax.ShapeDtypeStruct((dirs, S5_GB, rows_in, rows_st), BF16),
            jax.ShapeDtypeStruct((dirs, S5_GB, rows_in, rows_st), BF16),
            jax.ShapeDtypeStruct((dirs, S5_GB, rows_st, rows_in), BF16),
            jax.ShapeDtypeStruct((dirs, S5_GB, rows_st, rows_in), BF16),
            jax.ShapeDtypeStruct((dirs, g * p // V7X_LANES, V7X_LANES), F32),
            jax.ShapeDtypeStruct((dirs, g * p // V7X_LANES, V7X_LANES), F32),
        ],
        compiler_params=_cparams(("arbitrary", "arbitrary")),
        name="s5_params",
    )(*args)


def _pair_rows(lam):
    dirs, tiles, lanes = lam.shape
    half = V7X_SUBLANES // 2
    x = jnp.broadcast_to(lam.reshape(dirs, 2, tiles // 2, 1, lanes), (dirs, 2, tiles // 2, half, lanes))
    return jnp.transpose(x, (0, 2, 1, 3, 4)).reshape(dirs, tiles // 2, V7X_SUBLANES, lanes)


def _s5_scan_kernel(*refs, n_ctx_chunks, tq, pitch, reverse, ctx_row, emit_z):
    gb_per_half = S5_GB // 2
    n_in = 16 if emit_z else 13
    (hl_ref, hc_ref, nrm_ref, sh_ref, sc_ref, bre_ref, bim_ref, cre_ref, cim_ref, are_ref, aim_ref) = refs[:11]
    if emit_z:
        yl_ref, yc_ref, dsk_ref = refs[11:14]
    ol_ref, oc_ref = refs[n_in - 2:n_in]
    u_st = refs[n_in]
    s_re_k = refs[n_in + 1:n_in + 1 + gb_per_half]
    s_im_k = refs[n_in + 1 + gb_per_half:n_in + 1 + 2 * gb_per_half]
    h_re, h_im = refs[n_in + 1 + 2 * gb_per_half:]
    c = pl.program_id(0)
    n_b = hl_ref.shape[0]
    half_rows = n_b * pitch
    tiles_per_gb = s_re_k[0].shape[0]
    lanes_in = u_st.shape[1] // S5_GB

    @pl.when(c == 0)
    def _():
        h_re[...] = jnp.zeros_like(h_re)
        h_im[...] = jnp.zeros_like(h_im)
        u_st[...] = jnp.zeros_like(u_st)

    is_ctx = c < n_ctx_chunks

    @pl.when(is_ctx)
    def _():
        for b in range(n_b):
            u_st[b * pitch:b * pitch + tq, :] = _norm_mod(hc_ref[b], nrm_ref[...], sh_ref[ctx_row], sc_ref[ctx_row])

    @pl.when(jnp.logical_not(is_ctx))
    def _():
        for b in range(n_b):
            u_st[b * pitch:b * pitch + tq, :] = _norm_mod(hl_ref[b], nrm_ref[...], sh_ref[b], sc_ref[b])

    def project_in(gb):
        lhs = u_st[:, gb * lanes_in:(gb + 1) * lanes_in].astype(BF16)
        p_re = jnp.dot(lhs, bre_ref[gb], preferred_element_type=F32)
        p_im = jnp.dot(lhs, bim_ref[gb], preferred_element_type=F32)
        half = gb // gb_per_half
        rows = slice(half * half_rows, (half + 1) * half_rows)
        s_re, s_im = s_re_k[gb % gb_per_half], s_im_k[gb % gb_per_half]
        for j in range(tiles_per_gb):
            s_re[j, rows, :] = p_re[:, j * V7X_LANES:(j + 1) * V7X_LANES]
            s_im[j, rows, :] = p_im[:, j * V7X_LANES:(j + 1) * V7X_LANES]

    def scan_pass(k):
        s_re, s_im = s_re_k[k], s_im_k[k]
        ns = list(range(k * tiles_per_gb, (k + 1) * tiles_per_gb))
        a_r = [are_ref[n] for n in ns]
        a_i = [aim_ref[n] for n in ns]
        hr = [h_re[n] for n in ns]
        hi = [h_im[n] for n in ns]
        for t in range(tq):
            rows = pl.ds(tq - 1 - t if reverse else t, V7X_SUBLANES, stride=pitch)
            for j in range(tiles_per_gb):
                n_r = a_r[j] * hr[j] - a_i[j] * hi[j] + s_re[j, rows, :]
                n_i = a_r[j] * hi[j] + a_i[j] * hr[j] + s_im[j, rows, :]
                s_re[j, rows, :] = n_r
                s_im[j, rows, :] = n_i
                hr[j], hi[j] = n_r, n_i
        for j, n in enumerate(ns):
            h_re[n] = hr[j]
            h_im[n] = hi[j]

    def project_out(gb):
        half = gb // gb_per_half
        rows = slice(half * half_rows, (half + 1) * half_rows)
        s_re, s_im = s_re_k[gb % gb_per_half], s_im_k[gb % gb_per_half]
        l_re = jnp.concatenate([s_re[j, rows, :] for j in range(tiles_per_gb)], axis=1).astype(BF16)
        l_im = jnp.concatenate([s_im[j, rows, :] for j in range(tiles_per_gb)], axis=1).astype(BF16)
        return (jnp.dot(l_re, cre_ref[gb], preferred_element_type=F32)
                + jnp.dot(l_im, cim_ref[gb], preferred_element_type=F32))

    for gb in range(S5_GB):
        project_in(gb)
    for k in range(gb_per_half):
        scan_pass(k)
    ys = [project_out(gb) for gb in range(S5_GB)]

    def emit(out_ref, yf_ref):
        for gb in range(S5_GB):
            cols = slice(gb * lanes_in, (gb + 1) * lanes_in)
            for b in range(n_b):
                y = ys[gb][b * pitch:b * pitch + tq]
                if emit_z:
                    y = y + yf_ref[b, :, cols] + dsk_ref[:, cols] * u_st[b * pitch:b * pitch + tq, cols]
                    out_ref[b, :, cols] = _gelu_tanh(y).astype(out_ref.dtype)
                else:
                    out_ref[b, :, cols] = y

    @pl.when(is_ctx)
    def _():
        emit(oc_ref, yc_ref if emit_z else None)

    @pl.when(jnp.logical_not(is_ctx))
    def _():
        emit(ol_ref, yl_ref if emit_z else None)


def _gelu_tanh(x):
    return 0.5 * x * (1.0 + jnp.tanh(math.sqrt(2.0 / math.pi) * (x + 0.044715 * (x * x * x))))


def _s5_scan(h_lat, h_ctx, mods, layer, mod_base, norm_g, maps, direction, y_prev=None, d_skip=None):
    n_b, s_len, d = h_lat.shape
    l_len = h_ctx.shape[1]
    tq, pitch = S5_TQ, S5_PITCH
    n_l, n_s = l_len // tq, s_len // tq
    bd_bre, bd_bim, bd_cre, bd_cim, a_re, a_im = maps
    n_pairs = a_re.shape[1]
    reverse = direction == 1
    emit_z = y_prev is not None

    def lat_idx(c):
        k = jnp.maximum(c - n_l, 0)
        return n_s - 1 - k if reverse else k

    def ctx_idx(c):
        k = jnp.minimum(c, n_l - 1)
        return n_l - 1 - k if reverse else k

    def wspec(arr):
        shape = arr.shape[1:]
        return _resident((None,) + shape, lambda c: (direction,) + (0,) * len(shape))

    lat_spec = pl.BlockSpec((n_b, tq, d), lambda c: (0, lat_idx(c), 0))
    ctx_spec = pl.BlockSpec((n_b, tq, d), lambda c: (0, ctx_idx(c), 0))
    mod_rows = lambda col: pl.BlockSpec((None, COND_ROWS, 1, d), lambda c: (layer, 0, 0, col))
    in_specs = [lat_spec, ctx_spec,
                pl.BlockSpec((None, 1, d), lambda c: (layer, 0, 0)),
                mod_rows(mod_base), mod_rows(mod_base + 1),
                wspec(bd_bre), wspec(bd_bim), wspec(bd_cre), wspec(bd_cim), wspec(a_re), wspec(a_im)]
    args = [h_lat, h_ctx, norm_g.reshape(norm_g.shape[0], 1, d), mods, mods,
            bd_bre, bd_bim, bd_cre, bd_cim, a_re, a_im]
    out_dtype = F32
    if emit_z:
        in_specs += [lat_spec, ctx_spec, pl.BlockSpec((None, 1, d), lambda c: (0, 0, 0))]
        args += [y_prev[0], y_prev[1], d_skip.reshape(d_skip.shape[0], 1, d)]
        out_dtype = BF16
    kern = functools.partial(_s5_scan_kernel, n_ctx_chunks=n_l, tq=tq, pitch=pitch, reverse=reverse,
                             ctx_row=n_b, emit_z=emit_z)
    return pl.pallas_call(
        kern,
        grid=(n_l + n_s,),
        in_specs=in_specs,
        out_specs=[lat_spec, ctx_spec],
        out_shape=[jax.ShapeDtypeStruct((n_b, s_len, d), out_dtype),
                   jax.ShapeDtypeStruct((n_b, l_len, d), out_dtype)],
        scratch_shapes=[
            pltpu.VMEM((n_b * pitch, d), F32),
            *[pltpu.VMEM((2 * n_pairs // S5_GB, 2 * n_b * pitch, V7X_LANES), F32) for _ in range(S5_GB)],
            pltpu.VMEM((n_pairs, V7X_SUBLANES, V7X_LANES), F32),
            pltpu.VMEM((n_pairs, V7X_SUBLANES, V7X_LANES), F32),
        ],
        compiler_params=_cparams(("arbitrary",)),
        name="s5_bwd" if reverse else "s5_fwd",
    )(*args)


def _glu_kernel(z_ref, w1_ref, w2_ref, h_ref, gt_ref, o_ref):
    z = z_ref[...]
    for nb in range(o_ref.shape[1] // COL_BLK):
        cols = slice(nb * COL_BLK, (nb + 1) * COL_BLK)
        a = jnp.dot(z, w1_ref[:, cols], preferred_element_type=F32)
        b = jnp.dot(z, w2_ref[:, cols], preferred_element_type=F32)
        o_ref[:, cols] = h_ref[:, cols] + gt_ref[:, cols] * (a * jax.nn.sigmoid(b))


def _glu(z, w1, w2, sub, h, mods, layer, gate_col, *, tm, bidx):
    n_tok, d = h.shape
    return pl.pallas_call(
        _glu_kernel,
        grid=(n_tok // tm,),
        in_specs=[
            pl.BlockSpec((tm, d), lambda i: (i, 0)),
            _resident((None, d, d), lambda i: (sub, 0, 0)),
            _resident((None, d, d), lambda i: (sub, 0, 0)),
            pl.BlockSpec((tm, d), lambda i: (i, 0)),
            _mod_spec(layer, gate_col, d, bidx),
        ],
        out_specs=pl.BlockSpec((tm, d), lambda i: (i, 0)),
        out_shape=jax.ShapeDtypeStruct((n_tok, d), F32),
        compiler_params=_cparams(("arbitrary",)),
        name="s5_glu",
    )(z, w1, w2, h, mods)


def _rope_tables(seq):
    pairs = HEAD_DIM // 4
    freqs = ROPE_THETA ** (-jnp.arange(pairs, dtype=F32) / pairs)
    pos = jnp.arange(seq, dtype=jnp.int32)
    ang_r = (pos // GRID_W).astype(F32)[:, None] * freqs
    ang_c = (pos % GRID_W).astype(F32)[:, None] * freqs
    cos = jnp.concatenate([jnp.cos(ang_r), jnp.cos(ang_r), jnp.cos(ang_c), jnp.cos(ang_c)], axis=-1)
    sin = jnp.concatenate([-jnp.sin(ang_r), jnp.sin(ang_r), -jnp.sin(ang_c), jnp.sin(ang_c)], axis=-1)
    return cos, sin


def _qkv_kernel(h_ref, nrm_ref, sh_ref, sc_ref, w_ref, qg_ref, kg_ref, cos_ref, sin_ref, o_ref, m_ref,
                *, head0, n_q_heads, n_kv_heads, rope):
    def emit(rows, x, m):
        m_ref[rows, :] = m.astype(BF16)

    _norm_mod_blocks(h_ref, nrm_ref, sh_ref, sc_ref, emit)
    m = m_ref[...]
    lane = lax.broadcasted_iota(jnp.int32, (1, HEAD_DIM), 1)
    first = (lane % (HEAD_DIM // 2)) < (HEAD_DIM // 4)
    hpb = COL_BLK // HEAD_DIM
    for nb in range(w_ref.shape[1] // COL_BLK):
        acc = jnp.dot(m, w_ref[:, nb * COL_BLK:(nb + 1) * COL_BLK], preferred_element_type=F32)
        for hh in range(hpb):
            head = head0 + nb * hpb + hh
            x = acc[:, hh * HEAD_DIM:(hh + 1) * HEAD_DIM]
            if head < n_q_heads + n_kv_heads:
                gain = qg_ref[...] if head < n_q_heads else kg_ref[...]
                xn = x * lax.rsqrt(jnp.mean(x * x, axis=-1, keepdims=True) + EPS) * gain
                if rope:
                    partner = jnp.where(first, pltpu.roll(xn, HEAD_DIM - HEAD_DIM // 4, 1),
                                        pltpu.roll(xn, HEAD_DIM // 4, 1))
                    xn = xn * cos_ref[...] + partner * sin_ref[...]
                x = xn
            o_ref[nb * hpb + hh] = x.astype(BF16)


def _qkv(h, mods, layer, mod_base, norm_g, w_qkv, sub, q_gain, k_gain, cos, sin, *, seq, tm, bidx,
         n_q_heads, n_kv_heads, kv_only, rope):
    n_tok, d = h.shape
    n_b = n_tok // seq
    n_cols = w_qkv.shape[2]
    head0 = n_q_heads if kv_only else 0
    width = n_cols - head0 * HEAD_DIM
    heads = width // HEAD_DIM
    assert (head0 * HEAD_DIM) % width == 0
    tiles_per_seq = seq // tm
    kern = functools.partial(_qkv_kernel, head0=head0, n_q_heads=n_q_heads, n_kv_heads=n_kv_heads, rope=rope)
    return pl.pallas_call(
        kern,
        grid=(n_tok // tm,),
        in_specs=[
            pl.BlockSpec((tm, d), lambda i: (i, 0)),
            pl.BlockSpec((None, 1, d), lambda i: (layer, 0, 0)),
            _mod_spec(layer, mod_base + 0, d, bidx),
            _mod_spec(layer, mod_base + 1, d, bidx),
            _resident((None, d, width), lambda i: (sub, 0, head0 * HEAD_DIM // width)),
            pl.BlockSpec((None, 1, HEAD_DIM), lambda i: (sub, 0, 0)),
            pl.BlockSpec((None, 1, HEAD_DIM), lambda i: (sub, 0, 0)),
            pl.BlockSpec((tm, HEAD_DIM), lambda i: (i % tiles_per_seq, 0)),
            pl.BlockSpec((tm, HEAD_DIM), lambda i: (i % tiles_per_seq, 0)),
        ],
        out_specs=pl.BlockSpec((None, heads, tm, HEAD_DIM), lambda i: (i // tiles_per_seq, 0, i % tiles_per_seq, 0)),
        out_shape=jax.ShapeDtypeStruct((n_b, heads, seq, HEAD_DIM), BF16),
        scratch_shapes=[pltpu.VMEM((tm, d), BF16)],
        compiler_params=_cparams(("arbitrary",)),
        name="qkv",
    )(h, norm_g.reshape(norm_g.shape[0], 1, d), mods, mods, w_qkv,
      q_gain.reshape(q_gain.shape[0], 1, HEAD_DIM), k_gain.reshape(k_gain.shape[0], 1, HEAD_DIM), cos, sin)


def _attn_kernel(q_ref, kl_ref, vl_ref, kc_ref, vc_ref, o_ref):
    rep, tq, hd = q_ref.shape
    dn = (((1,), (1,)), ((), ()))
    c = (HEAD_DIM ** -0.5) * math.log2(math.e)
    chains = [(r, slice(sb * ATTN_ROWS, (sb + 1) * ATTN_ROWS)) for r in range(rep) for sb in range(tq // ATTN_ROWS)]

    def scores(chain):
        r, rows = chain
        q = q_ref[r, rows, :]
        return (lax.dot_general(q, kl_ref[...], dn, preferred_element_type=F32),
                lax.dot_general(q, kc_ref[...], dn, preferred_element_type=F32))

    ahead = [scores(ch) for ch in chains[:ATTN_AHEAD]]
    for i, (r, rows) in enumerate(chains):
        s_l, s_c = ahead.pop(0)
        if i + ATTN_AHEAD < len(chains):
            ahead.append(scores(chains[i + ATTN_AHEAD]))
        m = jnp.maximum(jnp.max(s_l, axis=-1, keepdims=True), jnp.max(s_c, axis=-1, keepdims=True))
        p_l = jnp.exp2((s_l - m) * c)
        p_c = jnp.exp2((s_c - m) * c)
        den = jnp.sum(p_l, axis=-1, keepdims=True) + jnp.sum(p_c, axis=-1, keepdims=True)
        o = (jnp.dot(p_l.astype(BF16), vl_ref[...], preferred_element_type=F32)
             + jnp.dot(p_c.astype(BF16), vc_ref[...], preferred_element_type=F32)) / den
        o_ref[rows, r * hd:(r + 1) * hd] = o.astype(BF16)


def _attention(qkv_lat, kv_ctx, n_q_heads, n_kv_heads, *, tq=256):
    n_b, _, seq, hd = qkv_lat.shape
    l_len = kv_ctx.shape[2]
    rep = n_q_heads // n_kv_heads
    return pl.pallas_call(
        _attn_kernel,
        grid=(n_b, n_kv_heads, seq // tq),
        in_specs=[
            pl.BlockSpec((None, rep, tq, hd), lambda b, g, i: (b, g, i, 0)),
            pl.BlockSpec((None, None, seq, hd), lambda b, g, i: (b, n_q_heads + g, 0, 0)),
            pl.BlockSpec((None, None, seq, hd), lambda b, g, i: (b, n_q_heads + n_kv_heads + g, 0, 0)),
            pl.BlockSpec((None, None, l_len, hd), lambda b, g, i: (b, g, 0, 0)),
            pl.BlockSpec((None, None, l_len, hd), lambda b, g, i: (b, n_kv_heads + g, 0, 0)),
        ],
        out_specs=pl.BlockSpec((None, tq, rep * hd), lambda b, g, i: (b, i, g)),
        out_shape=jax.ShapeDtypeStruct((n_b, seq, n_q_heads * hd), BF16),
        compiler_params=_cparams(("arbitrary", "arbitrary", "arbitrary")),
        name="attention",
    )(qkv_lat, qkv_lat, qkv_lat, kv_ctx, kv_ctx)


def _oproj_kernel(x_ref, w_ref, h_ref, gt_ref, o_ref):
    x = x_ref[...]
    for nb in range(o_ref.shape[1] // COL_BLK):
        cols = slice(nb * COL_BLK, (nb + 1) * COL_BLK)
        acc = jnp.dot(x, w_ref[:, cols], preferred_element_type=F32)
        o_ref[:, cols] = h_ref[:, cols] + gt_ref[:, cols] * acc


def _oproj(x, w_o, sub, h, mods, layer, gate_col, *, tm, bidx):
    n_tok, d = h.shape
    k_dim = x.shape[1]
    return pl.pallas_call(
        _oproj_kernel,
        grid=(n_tok // tm,),
        in_specs=[
            pl.BlockSpec((tm, k_dim), lambda i: (i, 0)),
            _resident((None, k_dim, d), lambda i: (sub, 0, 0)),
            pl.BlockSpec((tm, d), lambda i: (i, 0)),
            _mod_spec(layer, gate_col, d, bidx),
        ],
        out_specs=pl.BlockSpec((tm, d), lambda i: (i, 0)),
        out_shape=jax.ShapeDtypeStruct((n_tok, d), F32),
        compiler_params=_cparams(("arbitrary",)),
        name="oproj",
    )(x, w_o, h, mods)


def kernel(x, c, ctx, c_ctx, ada_w, ada_b, norm_ffn1, norm_mix, norm_ffn2, ffn1_w_gate, ffn1_w_up, ffn1_w_down, ffn2_w_gate, ffn2_w_up, ffn2_w_down, s5_a_re, s5_a_im, s5_log_dt, s5_b_re, s5_b_im, s5_c_re, s5_c_im, s5_d, s5_glu_w1, s5_glu_w2, attn_w_qkv, attn_w_o, attn_q_gain, attn_k_gain):
    n_b, seq, d = x.shape
    l_len = ctx.shape[1]
    depth = ada_w.shape[0]
    assert depth == 2 and ada_w.shape[2] == N_MOD * d
    assert n_b == 4, "the S5 scan packs 4 sequences x 2 lane tiles onto the 8 sublanes"
    n_q_heads = d // HEAD_DIM
    n_kv_heads = n_q_heads // KV_REP

    tm_lat = min(1024, seq)
    tm_ctx = min(1024, n_b * l_len)
    tm_mm = min(512, seq)
    tm_mm_ctx = min(512, l_len)
    ctx_row = n_b
    bidx_lat = lambda tm: (lambda i: (i * tm) // seq)
    bidx_ctx = lambda i: ctx_row

    cond = jnp.concatenate([c, c_ctx[None, :], jnp.zeros((COND_ROWS - n_b - 1, d), F32)], axis=0)
    mods = _ada_mods(cond, ada_w, ada_b).reshape(depth, COND_ROWS, 1, N_MOD * d)

    ffn1 = (ffn1_w_gate, ffn1_w_up, ffn1_w_down)
    ffn2 = (ffn2_w_gate, ffn2_w_up, ffn2_w_down)

    h_lat = x.reshape(n_b * seq, d)
    h_ctx = ctx.reshape(n_b * l_len, d)

    def ffn_pair(h_lat, h_ctx, layer, mod_base, norm_g, weights):
        h_ctx, w_bf16 = _ffn(h_ctx, mods, layer, mod_base, norm_g, *weights, tm=tm_ctx, bidx=bidx_ctx, emit_w=True)
        h_lat = _ffn(h_lat, mods, layer, mod_base, norm_g, *w_bf16, tm=tm_lat, bidx=bidx_lat(tm_lat))
        return h_lat, h_ctx

    layer = 0
    h_lat, h_ctx = ffn_pair(h_lat, h_ctx, layer, 0, norm_ffn1, ffn1)

    params = _s5_params(s5_a_re[0], s5_a_im[0], s5_log_dt[0], s5_b_re[0], s5_b_im[0], s5_c_re[0], s5_c_im[0])
    maps = tuple(params[:4]) + (_pair_rows(params[4]), _pair_rows(params[5]))
    h3_lat, h3_ctx = h_lat.reshape(n_b, seq, d), h_ctx.reshape(n_b, l_len, d)
    y_fwd = _s5_scan(h3_lat, h3_ctx, mods, layer, 3, norm_mix, maps, 0)
    z_lat, z_ctx = _s5_scan(h3_lat, h3_ctx, mods, layer, 3, norm_mix, maps, 1, y_prev=y_fwd, d_skip=s5_d)
    glu_w1, glu_w2 = s5_glu_w1.astype(BF16), s5_glu_w2.astype(BF16)
    h_lat = _glu(z_lat.reshape(n_b * seq, d), glu_w1, glu_w2, 0, h_lat, mods, layer, 5,
                 tm=tm_mm, bidx=bidx_lat(tm_mm))
    h_ctx = _glu(z_ctx.reshape(n_b * l_len, d), glu_w1, glu_w2, 0, h_ctx, mods, layer, 5,
                 tm=tm_mm_ctx, bidx=bidx_ctx)

    h_lat, h_ctx = ffn_pair(h_lat, h_ctx, layer, 6, norm_ffn2, ffn2)

    layer = 1
    h_lat, h_ctx = ffn_pair(h_lat, h_ctx, layer, 0, norm_ffn1, ffn1)

    cos, sin = _rope_tables(seq)
    w_qkv = attn_w_qkv.astype(BF16)
    qkv_lat = _qkv(h_lat, mods, layer, 3, norm_mix, w_qkv, 0, attn_q_gain, attn_k_gain, cos, sin,
                   seq=seq, tm=tm_mm, bidx=bidx_lat(tm_mm), n_q_heads=n_q_heads, n_kv_heads=n_kv_heads,
                   kv_only=False, rope=True)
    no_rope = jnp.zeros((l_len, HEAD_DIM), F32)
    kv_ctx = _qkv(h_ctx, mods, layer, 3, norm_mix, w_qkv, 0, attn_q_gain, attn_k_gain, no_rope, no_rope,
                  seq=l_len, tm=tm_mm_ctx, bidx=bidx_ctx, n_q_heads=n_q_heads, n_kv_heads=n_kv_heads,
                  kv_only=True, rope=False)
    o_lat = _attention(qkv_lat, kv_ctx, n_q_heads, n_kv_heads, tq=min(1024, seq))
    h_lat = _oproj(o_lat.reshape(n_b * seq, d), attn_w_o.astype(BF16), 0, h_lat, mods, layer, 5,
                   tm=tm_mm, bidx=bidx_lat(tm_mm))

    h_lat = _ffn(h_lat, mods, layer, 6, norm_ffn2, *ffn2, tm=tm_lat, bidx=bidx_lat(tm_lat))
    return h_lat.reshape(n_b, seq, d)
```

```python
import functools
import math

import jax
import jax.numpy as jnp
from jax import lax
from jax.experimental import pallas as pl
from jax.experimental.pallas import tpu as pltpu

F32 = jnp.float32
BF16 = jnp.bfloat16

EPS = 1e-6
N_MOD = 9
GRID_W = 64
HEAD_DIM = 128
KV_REP = 4
ROPE_THETA = 10000.0
S5_GROUP = 16
S5_STATE = 64

V7X_LANES = 128
V7X_SUBLANES = 8
V7X_VMEM_LIMIT = 60 * 1024 * 1024

S5_GB = 8
S5_TQ = 64
S5_PITCH = 68
COND_ROWS = 16
COL_BLK = 512
NORM_ROWS = 32
ATTN_AHEAD = 1
FFN_OUT_BLK = 512
ATTN_ROWS = 512


def _cparams(sem):
    return pltpu.CompilerParams(dimension_semantics=sem, vmem_limit_bytes=V7X_VMEM_LIMIT)


def _resident(shape, index):
    return pl.BlockSpec(shape, index, pipeline_mode=pl.Buffered(1))


def _ada_kernel(a_ref, w_ref, b_ref, o_ref):
    a = a_ref[...]
    s = (a * jax.nn.sigmoid(a)).astype(BF16)
    o_ref[...] = jnp.dot(s, w_ref[...].astype(BF16), preferred_element_type=F32) + b_ref[...]


def _ada_mods(cond, ada_w, ada_b, tn=1024):
    depth, d, n = ada_w.shape
    rows = cond.shape[0]
    return pl.pallas_call(
        _ada_kernel,
        grid=(depth, n // tn),
        in_specs=[
            pl.BlockSpec((rows, d), lambda l, j: (0, 0)),
            pl.BlockSpec((None, d, tn), lambda l, j: (l, 0, j)),
            pl.BlockSpec((None, 1, tn), lambda l, j: (l, 0, j)),
        ],
        out_specs=pl.BlockSpec((None, rows, tn), lambda l, j: (l, 0, j)),
        out_shape=jax.ShapeDtypeStruct((depth, rows, n), F32),
        compiler_params=_cparams(("arbitrary", "arbitrary")),
        name="ada_mods",
    )(cond, ada_w, ada_b.reshape(depth, 1, n))


def _mod_spec(layer, col, d, bidx):
    return pl.BlockSpec((None, None, 1, d), lambda i, *_: (layer, bidx(i), 0, col))


def _norm_mod(x, g, shift, scale):
    r = lax.rsqrt(jnp.mean(x * x, axis=-1, keepdims=True) + EPS)
    return (x * r * g) * (1.0 + scale) + shift


def _norm_mod_blocks(h_ref, nrm_ref, sh_ref, sc_ref, emit):
    g, shift, scale = nrm_ref[...], sh_ref[...], sc_ref[...]

    def body(i, carry):
        rows = pl.ds(pl.multiple_of(i * NORM_ROWS, NORM_ROWS), NORM_ROWS)
        x = h_ref[rows, :]
        emit(rows, x, _norm_mod(x, g, shift, scale))
        return carry

    lax.fori_loop(0, h_ref.shape[0] // NORM_ROWS, body, 0, unroll=4)


def _ffn_kernel(*refs, emit_w, n_side):
    h_ref, nrm_ref, sh_ref, sc_ref, gt_ref, wg_ref, wu_ref, wd_ref = refs[:8]
    side_in = refs[8:8 + n_side]
    o_ref = refs[8 + n_side]
    w_out = refs[9 + n_side:12 + n_side] if emit_w else ()
    side_out = refs[-1 - n_side:-1]
    m_ref = refs[-1]
    f = pl.program_id(1)

    @pl.when(f == 0)
    def _():
        def emit(rows, x, m):
            m_ref[rows, :] = m.astype(BF16)
            o_ref[rows, :] = x

        _norm_mod_blocks(h_ref, nrm_ref, sh_ref, sc_ref, emit)

    for src, dst in zip(side_in, side_out):
        dst[...] = src[...].astype(BF16)

    m = m_ref[...]
    wg, wu = wg_ref[...].astype(BF16), wu_ref[...].astype(BF16)
    if emit_w:
        w_out[0][...] = wg
        w_out[1][...] = wu
    g = jnp.dot(m, wg, preferred_element_type=F32)
    u = jnp.dot(m, wu, preferred_element_type=F32)
    a = (g * jax.nn.sigmoid(g) * u).astype(BF16)
    for nb in range(o_ref.shape[1] // FFN_OUT_BLK):
        cols = slice(nb * FFN_OUT_BLK, (nb + 1) * FFN_OUT_BLK)
        wd = wd_ref[:, cols].astype(BF16)
        if emit_w:
            w_out[2][:, cols] = wd
        p = jnp.dot(a, wd, preferred_element_type=F32)
        o_ref[:, cols] += (0.5 * gt_ref[:, cols]) * p


def _ffn(h, mods, layer, mod_base, norm_g, w_gate, w_up, w_down, *, tm, bidx, emit_w=False, cast_next=None):
    n_tok, d = h.shape
    f_dim = w_gate.shape[-1]
    if w_gate.ndim == 3:
        tf = 256
        wspec = lambda shape, idx: pl.BlockSpec((None,) + shape, lambda i, f: (layer,) + idx(f))
    else:
        assert not emit_w and w_gate.dtype == BF16
        tf = 512
        wspec = lambda shape, idx: pl.BlockSpec(shape, lambda i, f: idx(f))
    n_i, n_f = n_tok // tm, f_dim // tf
    out_specs = [pl.BlockSpec((tm, d), lambda i, f: (i, 0))]
    out_shape = [jax.ShapeDtypeStruct((n_tok, d), F32)]
    w_shapes = [jax.ShapeDtypeStruct((d, f_dim), BF16), jax.ShapeDtypeStruct((d, f_dim), BF16),
                jax.ShapeDtypeStruct((f_dim, d), BF16)]
    if emit_w:
        assert n_i == 1, "each weight tile must be visited exactly once"
        out_specs += [pl.BlockSpec((d, tf), lambda i, f: (0, f)), pl.BlockSpec((d, tf), lambda i, f: (0, f)),
                      pl.BlockSpec((tf, d), lambda i, f: (f, 0))]
        out_shape += w_shapes
    side_args, side_specs = [], []
    if cast_next is not None:
        assert not emit_w
        side_args, s_layer = list(cast_next[0]), cast_next[1]
        rb, cb = d // n_i, f_dim // n_f
        side_specs = [pl.BlockSpec((None, rb, cb), lambda i, f: (s_layer, i, f)),
                      pl.BlockSpec((None, rb, cb), lambda i, f: (s_layer, i, f)),
                      pl.BlockSpec((None, cb, rb), lambda i, f: (s_layer, f, i))]
        out_specs += [pl.BlockSpec((rb, cb), lambda i, f: (i, f)), pl.BlockSpec((rb, cb), lambda i, f: (i, f)),
                      pl.BlockSpec((cb, rb), lambda i, f: (f, i))]
        out_shape += w_shapes
    out = pl.pallas_call(
        functools.partial(_ffn_kernel, emit_w=emit_w, n_side=len(side_args)),
        grid=(n_i, n_f),
        in_specs=[
            pl.BlockSpec((tm, d), lambda i, f: (i, 0), pipeline_mode=pl.Buffered(1)),
            pl.BlockSpec((None, 1, d), lambda i, f: (layer, 0, 0)),
            _mod_spec(layer, mod_base + 0, d, bidx),
            _mod_spec(layer, mod_base + 1, d, bidx),
            _mod_spec(layer, mod_base + 2, d, bidx),
            wspec((d, tf), lambda f: (0, f)),
            wspec((d, tf), lambda f: (0, f)),
            wspec((tf, d), lambda f: (f, 0)),
        ] + side_specs,
        out_specs=out_specs,
        out_shape=out_shape,
        scratch_shapes=[pltpu.VMEM((tm, d), BF16)],
        compiler_params=_cparams(("arbitrary", "arbitrary")),
        name="ffn",
    )(h, norm_g.reshape(norm_g.shape[0], 1, d), mods, mods, mods, w_gate, w_up, w_down, *side_args)
    return (out[0], tuple(out[1:])) if len(out) > 1 else out[0]


def _zoh(a_re, a_im, log_dt):
    dt = jnp.exp(log_dt)
    mag = jnp.exp(a_re * dt)
    l_re = mag * jnp.cos(a_im * dt)
    l_im = mag * jnp.sin(a_im * dt)
    den = a_re * a_re + a_im * a_im
    c_re = ((l_re - 1.0) * a_re + l_im * a_im) / den
    c_im = (l_im * a_re - (l_re - 1.0) * a_im) / den
    return l_re, l_im, c_re, c_im


def _lane_tile(x, reps):
    w = x.shape[1]
    row = lax.broadcasted_iota(jnp.int32, (w, w * reps), 0)
    col = lax.broadcasted_iota(jnp.int32, (w, w * reps), 1)
    sel = (row == col % w).astype(BF16)
    return jnp.dot(x, sel, preferred_element_type=F32)


def _s5_param_kernel(are_ref, aim_ref, ldt_ref, bre_ref, bim_ref, cre_ref, cim_ref, ar2_ref, ai2_ref, ld2_ref,
                     obr_ref, obi_ref, ocr_ref, oci_ref, olr_ref, oli_ref):
    gpb = are_ref.shape[0] // S5_GROUP
    _, _, k_re, k_im = _zoh(are_ref[...], aim_ref[...], ldt_ref[...])
    bb_re = k_re * bre_ref[...] - k_im * bim_ref[...]
    bb_im = k_re * bim_ref[...] + k_im * bre_ref[...]
    n_in, n_st = bb_re.shape[0], S5_STATE * gpb
    on_diag = (lax.broadcasted_iota(jnp.int32, (n_in, n_st), 0) // S5_GROUP
               == lax.broadcasted_iota(jnp.int32, (n_in, n_st), 1) // S5_STATE)
    obr_ref[...] = jnp.where(on_diag, _lane_tile(bb_re.astype(BF16), gpb), 0.0).astype(BF16)
    obi_ref[...] = jnp.where(on_diag, _lane_tile(bb_im.astype(BF16), gpb), 0.0).astype(BF16)
    on_diag_t = (lax.broadcasted_iota(jnp.int32, (n_st, n_in), 0) // S5_STATE
                 == lax.broadcasted_iota(jnp.int32, (n_st, n_in), 1) // S5_GROUP)
    ocr_ref[...] = jnp.where(on_diag_t, _lane_tile(cre_ref[...].astype(BF16), gpb), 0.0).astype(BF16)
    oci_ref[...] = jnp.where(on_diag_t, _lane_tile((-cim_ref[...]).astype(BF16), gpb), 0.0).astype(BF16)
    l_re, l_im, _, _ = _zoh(ar2_ref[...], ai2_ref[...], ld2_ref[...])
    olr_ref[...] = l_re
    oli_ref[...] = l_im


def _s5_params(a_re, a_im, log_dt, b_re, b_im, c_re, c_im):
    dirs, g, p = a_re.shape
    hch = b_re.shape[3]
    gpb = g // S5_GB
    rows_in, rows_st = gpb * hch, gpb * p

    def per_channel(v):
        return jnp.broadcast_to(v[:, :, None, :], (dirs, g, hch, p)).reshape(dirs, S5_GB, rows_in, p)

    ldt = jnp.broadcast_to(log_dt[:, :, None], (dirs, g, p))
    args = (
        per_channel(a_re), per_channel(a_im), per_channel(ldt),
        jnp.swapaxes(b_re, 2, 3).reshape(dirs, S5_GB, rows_in, p),
        jnp.swapaxes(b_im, 2, 3).reshape(dirs, S5_GB, rows_in, p),
        jnp.swapaxes(c_re, 2, 3).reshape(dirs, S5_GB, rows_st, hch),
        jnp.swapaxes(c_im, 2, 3).reshape(dirs, S5_GB, rows_st, hch),
        a_re.reshape(dirs, g * p // V7X_LANES, V7X_LANES),
        a_im.reshape(dirs, g * p // V7X_LANES, V7X_LANES),
        ldt.reshape(dirs, g * p // V7X_LANES, V7X_LANES),
    )
    blk = lambda r, c: pl.BlockSpec((None, None, r, c), lambda dd, gb: (dd, gb, 0, 0))
    flat = pl.BlockSpec((None, g * p // V7X_LANES, V7X_LANES), lambda dd, gb: (dd, 0, 0))
    return pl.pallas_call(
        _s5_param_kernel,
        grid=(dirs, S5_GB),
        in_specs=[blk(rows_in, p)] * 5 + [blk(rows_st, hch)] * 2 + [flat] * 3,
        out_specs=[blk(rows_in, rows_st), blk(rows_in, rows_st), blk(rows_st, rows_in), blk(rows_st, rows_in),
                   flat, flat],
        out_shape=[
            jax.ShapeDtypeStruct((dirs, S5_GB, rows_in, rows_st), BF16),
            jax.ShapeDtypeStruct((dirs, S5_GB, rows_in, rows_st), BF16),
            jax.ShapeDtypeStruct((dirs, S5_GB, rows_st, rows_in), BF16),
            jax.ShapeDtypeStruct((dirs, S5_GB, rows_st, rows_in), BF16),
            jax.ShapeDtypeStruct((dirs, g * p // V7X_LANES, V7X_LANES), F32),
            jax.ShapeDtypeStruct((dirs, g * p // V7X_LANES, V7X_LANES), F32),
        ],
        compiler_params=_cparams(("arbitrary", "arbitrary")),
        name="s5_params",
    )(*args)


def _pair_rows(lam):
    dirs, tiles, lanes = lam.shape
    half = V7X_SUBLANES // 2
    x = jnp.broadcast_to(lam.reshape(dirs, 2, tiles // 2, 1, lanes), (dirs, 2, tiles // 2, half, lanes))
    return jnp.transpose(x, (0, 2, 1, 3, 4)).reshape(dirs, tiles // 2, V7X_SUBLANES, lanes)


def _s5_scan_kernel(*refs, n_ctx_chunks, tq, pitch, reverse, ctx_row, emit_z):
    gb_per_half = S5_GB // 2
    n_in = 16 if emit_z else 13
    (hl_ref, hc_ref, nrm_ref, sh_ref, sc_ref, bre_ref, bim_ref, cre_ref, cim_ref, are_ref, aim_ref) = refs[:11]
    if emit_z:
        yl_ref, yc_ref, dsk_ref = refs[11:14]
    ol_ref, oc_ref = refs[n_in - 2:n_in]
    u_st = refs[n_in]
    s_re_k = refs[n_in + 1:n_in + 1 + gb_per_half]
    s_im_k = refs[n_in + 1 + gb_per_half:n_in + 1 + 2 * gb_per_half]
    h_re, h_im = refs[n_in + 1 + 2 * gb_per_half:]
    c = pl.program_id(0)
    n_b = hl_ref.shape[0]
    half_rows = n_b * pitch
    tiles_per_gb = s_re_k[0].shape[0]
    lanes_in = u_st.shape[1] // S5_GB

    @pl.when(c == 0)
    def _():
        h_re[...] = jnp.zeros_like(h_re)
        h_im[...] = jnp.zeros_like(h_im)
        u_st[...] = jnp.zeros_like(u_st)

    is_ctx = c < n_ctx_chunks

    @pl.when(is_ctx)
    def _():
        for b in range(n_b):
            u_st[b * pitch:b * pitch + tq, :] = _norm_mod(hc_ref[b], nrm_ref[...], sh_ref[ctx_row], sc_ref[ctx_row])

    @pl.when(jnp.logical_not(is_ctx))
    def _():
        for b in range(n_b):
            u_st[b * pitch:b * pitch + tq, :] = _norm_mod(hl_ref[b], nrm_ref[...], sh_ref[b], sc_ref[b])

    def project_in(gb):
        lhs = u_st[:, gb * lanes_in:(gb + 1) * lanes_in].astype(BF16)
        p_re = jnp.dot(lhs, bre_ref[gb], preferred_element_type=F32)
        p_im = jnp.dot(lhs, bim_ref[gb], preferred_element_type=F32)
        half = gb // gb_per_half
        rows = slice(half * half_rows, (half + 1) * half_rows)
        s_re, s_im = s_re_k[gb % gb_per_half], s_im_k[gb % gb_per_half]
        for j in range(tiles_per_gb):
            s_re[j, rows, :] = p_re[:, j * V7X_LANES:(j + 1) * V7X_LANES]
            s_im[j, rows, :] = p_im[:, j * V7X_LANES:(j + 1) * V7X_LANES]

    def scan_pass(k):
        s_re, s_im = s_re_k[k], s_im_k[k]
        ns = list(range(k * tiles_per_gb, (k + 1) * tiles_per_gb))
        a_r = [are_ref[n] for n in ns]
        a_i = [aim_ref[n] for n in ns]
        hr = [h_re[n] for n in ns]
        hi = [h_im[n] for n in ns]
        for t in range(tq):
            rows = pl.ds(tq - 1 - t if reverse else t, V7X_SUBLANES, stride=pitch)
            for j in range(tiles_per_gb):
                n_r = a_r[j] * hr[j] - a_i[j] * hi[j] + s_re[j, rows, :]
                n_i = a_r[j] * hi[j] + a_i[j] * hr[j] + s_im[j, rows, :]
                s_re[j, rows, :] = n_r
                s_im[j, rows, :] = n_i
                hr[j], hi[j] = n_r, n_i
        for j, n in enumerate(ns):
            h_re[n] = hr[j]
            h_im[n] = hi[j]

    def project_out(gb):
        half = gb // gb_per_half
        rows = slice(half * half_rows, (half + 1) * half_rows)
        s_re, s_im = s_re_k[gb % gb_per_half], s_im_k[gb % gb_per_half]
        l_re = jnp.concatenate([s_re[j, rows, :] for j in range(tiles_per_gb)], axis=1).astype(BF16)
        l_im = jnp.concatenate([s_im[j, rows, :] for j in range(tiles_per_gb)], axis=1).astype(BF16)
        return (jnp.dot(l_re, cre_ref[gb], preferred_element_type=F32)
                + jnp.dot(l_im, cim_ref[gb], preferred_element_type=F32))

    for gb in range(S5_GB):
        project_in(gb)
    for k in range(gb_per_half):
        scan_pass(k)
    ys = [project_out(gb) for gb in range(S5_GB)]

    def emit(out_ref, yf_ref):
        for gb in range(S5_GB):
            cols = slice(gb * lanes_in, (gb + 1) * lanes_in)
            for b in range(n_b):
                y = ys[gb][b * pitch:b * pitch + tq]
                if emit_z:
                    y = y + yf_ref[b, :, cols] + dsk_ref[:, cols] * u_st[b * pitch:b * pitch + tq, cols]
                    out_ref[b, :, cols] = _gelu_tanh(y).astype(out_ref.dtype)
                else:
                    out_ref[b, :, cols] = y

    @pl.when(is_ctx)
    def _():
        emit(oc_ref, yc_ref if emit_z else None)

    @pl.when(jnp.logical_not(is_ctx))
    def _():
        emit(ol_ref, yl_ref if emit_z else None)


def _gelu_tanh(x):
    return 0.5 * x * (1.0 + jnp.tanh(math.sqrt(2.0 / math.pi) * (x + 0.044715 * (x * x * x))))


def _s5_scan(h_lat, h_ctx, mods, layer, mod_base, norm_g, maps, direction, y_prev=None, d_skip=None):
    n_b, s_len, d = h_lat.shape
    l_len = h_ctx.shape[1]
    tq, pitch = S5_TQ, S5_PITCH
    n_l, n_s = l_len // tq, s_len // tq
    bd_bre, bd_bim, bd_cre, bd_cim, a_re, a_im = maps
    n_pairs = a_re.shape[1]
    reverse = direction == 1
    emit_z = y_prev is not None

    def lat_idx(c):
        k = jnp.maximum(c - n_l, 0)
        return n_s - 1 - k if reverse else k

    def ctx_idx(c):
        k = jnp.minimum(c, n_l - 1)
        return n_l - 1 - k if reverse else k

    def wspec(arr):
        shape = arr.shape[1:]
        return _resident((None,) + shape, lambda c: (direction,) + (0,) * len(shape))

    lat_spec = pl.BlockSpec((n_b, tq, d), lambda c: (0, lat_idx(c), 0))
    ctx_spec = pl.BlockSpec((n_b, tq, d), lambda c: (0, ctx_idx(c), 0))
    mod_rows = lambda col: pl.BlockSpec((None, COND_ROWS, 1, d), lambda c: (layer, 0, 0, col))
    in_specs = [lat_spec, ctx_spec,
                pl.BlockSpec((None, 1, d), lambda c: (layer, 0, 0)),
                mod_rows(mod_base), mod_rows(mod_base + 1),
                wspec(bd_bre), wspec(bd_bim), wspec(bd_cre), wspec(bd_cim), wspec(a_re), wspec(a_im)]
    args = [h_lat, h_ctx, norm_g.reshape(norm_g.shape[0], 1, d), mods, mods,
            bd_bre, bd_bim, bd_cre, bd_cim, a_re, a_im]
    out_dtype = F32
    if emit_z:
        in_specs += [lat_spec, ctx_spec, pl.BlockSpec((None, 1, d), lambda c: (0, 0, 0))]
        args += [y_prev[0], y_prev[1], d_skip.reshape(d_skip.shape[0], 1, d)]
        out_dtype = BF16
    kern = functools.partial(_s5_scan_kernel, n_ctx_chunks=n_l, tq=tq, pitch=pitch, reverse=reverse,
                             ctx_row=n_b, emit_z=emit_z)
    return pl.pallas_call(
        kern,
        grid=(n_l + n_s,),
        in_specs=in_specs,
        out_specs=[lat_spec, ctx_spec],
        out_shape=[jax.ShapeDtypeStruct((n_b, s_len, d), out_dtype),
                   jax.ShapeDtypeStruct((n_b, l_len, d), out_dtype)],
        scratch_shapes=[
            pltpu.VMEM((n_b * pitch, d), F32),
            *[pltpu.VMEM((2 * n_pairs // S5_GB, 2 * n_b * pitch, V7X_LANES), F32) for _ in range(S5_GB)],
            pltpu.VMEM((n_pairs, V7X_SUBLANES, V7X_LANES), F32),
            pltpu.VMEM((n_pairs, V7X_SUBLANES, V7X_LANES), F32),
        ],
        compiler_params=_cparams(("arbitrary",)),
        name="s5_bwd" if reverse else "s5_fwd",
    )(*args)


def _glu_kernel(z_ref, w1_ref, w2_ref, h_ref, gt_ref, o_ref):
    z = z_ref[...]
    for nb in range(o_ref.shape[1] // COL_BLK):
        cols = slice(nb * COL_BLK, (nb + 1) * COL_BLK)
        a = jnp.dot(z, w1_ref[:, cols], preferred_element_type=F32)
        b = jnp.dot(z, w2_ref[:, cols], preferred_element_type=F32)
        o_ref[:, cols] = h_ref[:, cols] + gt_ref[:, cols] * (a * jax.nn.sigmoid(b))


def _glu(z, w1, w2, sub, h, mods, layer, gate_col, *, tm, bidx):
    n_tok, d = h.shape
    return pl.pallas_call(
        _glu_kernel,
        grid=(n_tok // tm,),
        in_specs=[
            pl.BlockSpec((tm, d), lambda i: (i, 0)),
            _resident((None, d, d), lambda i: (sub, 0, 0)),
            _resident((None, d, d), lambda i: (sub, 0, 0)),
            pl.BlockSpec((tm, d), lambda i: (i, 0)),
            _mod_spec(layer, gate_col, d, bidx),
        ],
        out_specs=pl.BlockSpec((tm, d), lambda i: (i, 0)),
        out_shape=jax.ShapeDtypeStruct((n_tok, d), F32),
        compiler_params=_cparams(("arbitrary",)),
        name="s5_glu",
    )(z, w1, w2, h, mods)


def _rope_tables(seq):
    pairs = HEAD_DIM // 4
    freqs = ROPE_THETA ** (-jnp.arange(pairs, dtype=F32) / pairs)
    pos = jnp.arange(seq, dtype=jnp.int32)
    ang_r = (pos // GRID_W).astype(F32)[:, None] * freqs
    ang_c = (pos % GRID_W).astype(F32)[:, None] * freqs
    cos = jnp.concatenate([jnp.cos(ang_r), jnp.cos(ang_r), jnp.cos(ang_c), jnp.cos(ang_c)], axis=-1)
    sin = jnp.concatenate([-jnp.sin(ang_r), jnp.sin(ang_r), -jnp.sin(ang_c), jnp.sin(ang_c)], axis=-1)
    return cos, sin


def _qkv_kernel(h_ref, nrm_ref, sh_ref, sc_ref, w_ref, qg_ref, kg_ref, cos_ref, sin_ref, o_ref,
                *, head0, n_q_heads, n_kv_heads, rope):
    m = _norm_mod(h_ref[...], nrm_ref[...], sh_ref[...], sc_ref[...]).astype(BF16)
    lane = lax.broadcasted_iota(jnp.int32, (1, HEAD_DIM), 1)
    first = (lane % (HEAD_DIM // 2)) < (HEAD_DIM // 4)
    hpb = COL_BLK // HEAD_DIM
    for nb in range(w_ref.shape[1] // COL_BLK):
        acc = jnp.dot(m, w_ref[:, nb * COL_BLK:(nb + 1) * COL_BLK], preferred_element_type=F32)
        for hh in range(hpb):
            head = head0 + nb * hpb + hh
            x = acc[:, hh * HEAD_DIM:(hh + 1) * HEAD_DIM]
            if head < n_q_heads + n_kv_heads:
                gain = qg_ref[...] if head < n_q_heads else kg_ref[...]
                xn = x * lax.rsqrt(jnp.mean(x * x, axis=-1, keepdims=True) + EPS) * gain
                if rope:
                    partner = jnp.where(first, pltpu.roll(xn, HEAD_DIM - HEAD_DIM // 4, 1),
                                        pltpu.roll(xn, HEAD_DIM // 4, 1))
                    xn = xn * cos_ref[...] + partner * sin_ref[...]
                x = xn
            o_ref[nb * hpb + hh] = x.astype(BF16)


def _qkv(h, mods, layer, mod_base, norm_g, w_qkv, sub, q_gain, k_gain, cos, sin, *, seq, tm, bidx,
         n_q_heads, n_kv_heads, kv_only, rope):
    n_tok, d = h.shape
    n_b = n_tok // seq
    n_cols = w_qkv.shape[2]
    head0 = n_q_heads if kv_only else 0
    width = n_cols - head0 * HEAD_DIM
    heads = width // HEAD_DIM
    assert (head0 * HEAD_DIM) % width == 0
    tiles_per_seq = seq // tm
    kern = functools.partial(_qkv_kernel, head0=head0, n_q_heads=n_q_heads, n_kv_heads=n_kv_heads, rope=rope)
    return pl.pallas_call(
        kern,
        grid=(n_tok // tm,),
        in_specs=[
            pl.BlockSpec((tm, d), lambda i: (i, 0)),
            pl.BlockSpec((None, 1, d), lambda i: (layer, 0, 0)),
            _mod_spec(layer, mod_base + 0, d, bidx),
            _mod_spec(layer, mod_base + 1, d, bidx),
            _resident((None, d, width), lambda i: (sub, 0, head0 * HEAD_DIM // width)),
            pl.BlockSpec((None, 1, HEAD_DIM), lambda i: (sub, 0, 0)),
            pl.BlockSpec((None, 1, HEAD_DIM), lambda i: (sub, 0, 0)),
            pl.BlockSpec((tm, HEAD_DIM), lambda i: (i % tiles_per_seq, 0)),
            pl.BlockSpec((tm, HEAD_DIM), lambda i: (i % tiles_per_seq, 0)),
        ],
        out_specs=pl.BlockSpec((None, heads, tm, HEAD_DIM), lambda i: (i // tiles_per_seq, 0, i % tiles_per_seq, 0)),
        out_shape=jax.ShapeDtypeStruct((n_b, heads, seq, HEAD_DIM), BF16),
        compiler_params=_cparams(("arbitrary",)),
        name="qkv",
    )(h, norm_g.reshape(norm_g.shape[0], 1, d), mods, mods, w_qkv,
      q_gain.reshape(q_gain.shape[0], 1, HEAD_DIM), k_gain.reshape(k_gain.shape[0], 1, HEAD_DIM), cos, sin)


def _attn_kernel(q_ref, kl_ref, vl_ref, kc_ref, vc_ref, o_ref):
    rep, tq, hd = q_ref.shape
    dn = (((1,), (1,)), ((), ()))
    c = (HEAD_DIM ** -0.5) * math.log2(math.e)
    blk = min(ATTN_ROWS, tq)
    chains = [(r, slice(sb * blk, (sb + 1) * blk)) for r in range(rep) for sb in range(tq // blk)]

    def scores(chain):
        r, rows = chain
        q = q_ref[r, rows, :]
        return (lax.dot_general(q, kl_ref[...], dn, preferred_element_type=F32),
                lax.dot_general(q, kc_ref[...], dn, preferred_element_type=F32))

    ahead = [scores(ch) for ch in chains[:ATTN_AHEAD]]
    for i, (r, rows) in enumerate(chains):
        s_l, s_c = ahead.pop(0)
        if i + ATTN_AHEAD < len(chains):
            ahead.append(scores(chains[i + ATTN_AHEAD]))
        m = jnp.maximum(jnp.max(s_l, axis=-1, keepdims=True), jnp.max(s_c, axis=-1, keepdims=True))
        p_l = jnp.exp2((s_l - m) * c)
        p_c = jnp.exp2((s_c - m) * c)
        den = jnp.sum(p_l, axis=-1, keepdims=True) + jnp.sum(p_c, axis=-1, keepdims=True)
        o = (jnp.dot(p_l.astype(BF16), vl_ref[...], preferred_element_type=F32)
             + jnp.dot(p_c.astype(BF16), vc_ref[...], preferred_element_type=F32)) / den
        o_ref[rows, r * hd:(r + 1) * hd] = o.astype(BF16)


def _attention(qkv_lat, kv_ctx, n_q_heads, n_kv_heads, *, tq=256):
    n_b, _, seq, hd = qkv_lat.shape
    l_len = kv_ctx.shape[2]
    rep = n_q_heads // n_kv_heads
    return pl.pallas_call(
        _attn_kernel,
        grid=(n_b, n_kv_heads, seq // tq),
        in_specs=[
            pl.BlockSpec((None, rep, tq, hd), lambda b, g, i: (b, g, i, 0)),
            pl.BlockSpec((None, None, seq, hd), lambda b, g, i: (b, n_q_heads + g, 0, 0)),
            pl.BlockSpec((None, None, seq, hd), lambda b, g, i: (b, n_q_heads + n_kv_heads + g, 0, 0)),
            pl.BlockSpec((None, None, l_len, hd), lambda b, g, i: (b, g, 0, 0)),
            pl.BlockSpec((None, None, l_len, hd), lambda b, g, i: (b, n_kv_heads + g, 0, 0)),
        ],
        out_specs=pl.BlockSpec((None, tq, rep * hd), lambda b, g, i: (b, i, g)),
        out_shape=jax.ShapeDtypeStruct((n_b, seq, n_q_heads * hd), BF16),
        compiler_params=_cparams(("arbitrary", "arbitrary", "arbitrary")),
        name="attention",
    )(qkv_lat, qkv_lat, qkv_lat, kv_ctx, kv_ctx)


def _oproj_kernel(x_ref, w_ref, h_ref, gt_ref, o_ref):
    x = x_ref[...]
    for nb in range(o_ref.shape[1] // COL_BLK):
        cols = slice(nb * COL_BLK, (nb + 1) * COL_BLK)
        acc = jnp.dot(x, w_ref[:, cols], preferred_element_type=F32)
        o_ref[:, cols] = h_ref[:, cols] + gt_ref[:, cols] * acc


def _oproj(x, w_o, sub, h, mods, layer, gate_col, *, tm, bidx):
    n_tok, d = h.shape
    k_dim = x.shape[1]
    return pl.pallas_call(
        _oproj_kernel,
        grid=(n_tok // tm,),
        in_specs=[
            pl.BlockSpec((tm, k_dim), lambda i: (i, 0)),
            _resident((None, k_dim, d), lambda i: (sub, 0, 0)),
            pl.BlockSpec((tm, d), lambda i: (i, 0)),
            _mod_spec(layer, gate_col, d, bidx),
        ],
        out_specs=pl.BlockSpec((tm, d), lambda i: (i, 0)),
        out_shape=jax.ShapeDtypeStruct((n_tok, d), F32),
        compiler_params=_cparams(("arbitrary",)),
        name="oproj",
    )(x, w_o, h, mods)


def kernel(x, c, ctx, c_ctx, ada_w, ada_b, norm_ffn1, norm_mix, norm_ffn2, ffn1_w_gate, ffn1_w_up, ffn1_w_down, ffn2_w_gate, ffn2_w_up, ffn2_w_down, s5_a_re, s5_a_im, s5_log_dt, s5_b_re, s5_b_im, s5_c_re, s5_c_im, s5_d, s5_glu_w1, s5_glu_w2, attn_w_qkv, attn_w_o, attn_q_gain, attn_k_gain):
    n_b, seq, d = x.shape
    l_len = ctx.shape[1]
    depth = ada_w.shape[0]
    assert depth == 2 and ada_w.shape[2] == N_MOD * d
    assert n_b == 4, "the S5 scan packs 4 sequences x 2 lane tiles onto the 8 sublanes"
    n_q_heads = d // HEAD_DIM
    n_kv_heads = n_q_heads // KV_REP

    tm_lat = min(1024, seq)
    tm_ctx = min(1024, n_b * l_len)
    tm_mm = min(512, seq)
    tm_mm_ctx = min(512, l_len)
    ctx_row = n_b
    bidx_lat = lambda tm: (lambda i: (i * tm) // seq)
    bidx_ctx = lambda i: ctx_row

    cond = jnp.concatenate([c, c_ctx[None, :], jnp.zeros((COND_ROWS - n_b - 1, d), F32)], axis=0)
    mods = _ada_mods(cond, ada_w, ada_b).reshape(depth, COND_ROWS, 1, N_MOD * d)

    ffn1 = (ffn1_w_gate, ffn1_w_up, ffn1_w_down)
    ffn2 = (ffn2_w_gate, ffn2_w_up, ffn2_w_down)

    h_lat = x.reshape(n_b * seq, d)
    h_ctx = ctx.reshape(n_b * l_len, d)

    def ffn_pair(h_lat, h_ctx, layer, mod_base, norm_g, weights, cast_next=None):
        h_ctx, w_bf16 = _ffn(h_ctx, mods, layer, mod_base, norm_g, *weights, tm=tm_ctx, bidx=bidx_ctx, emit_w=True)
        lat = _ffn(h_lat, mods, layer, mod_base, norm_g, *w_bf16, tm=tm_lat, bidx=bidx_lat(tm_lat),
                   cast_next=cast_next)
        return (lat, h_ctx) if cast_next is None else (lat[0], h_ctx, lat[1])

    layer = 0
    h_lat, h_ctx = ffn_pair(h_lat, h_ctx, layer, 0, norm_ffn1, ffn1)

    params = _s5_params(s5_a_re[0], s5_a_im[0], s5_log_dt[0], s5_b_re[0], s5_b_im[0], s5_c_re[0], s5_c_im[0])
    maps = tuple(params[:4]) + (_pair_rows(params[4]), _pair_rows(params[5]))
    h3_lat, h3_ctx = h_lat.reshape(n_b, seq, d), h_ctx.reshape(n_b, l_len, d)
    y_fwd = _s5_scan(h3_lat, h3_ctx, mods, layer, 3, norm_mix, maps, 0)
    z_lat, z_ctx = _s5_scan(h3_lat, h3_ctx, mods, layer, 3, norm_mix, maps, 1, y_prev=y_fwd, d_skip=s5_d)
    glu_w1, glu_w2 = s5_glu_w1.astype(BF16), s5_glu_w2.astype(BF16)
    h_lat = _glu(z_lat.reshape(n_b * seq, d), glu_w1, glu_w2, 0, h_lat, mods, layer, 5,
                 tm=tm_mm, bidx=bidx_lat(tm_mm))
    h_ctx = _glu(z_ctx.reshape(n_b * l_len, d), glu_w1, glu_w2, 0, h_ctx, mods, layer, 5,
                 tm=tm_mm_ctx, bidx=bidx_ctx)

    h_lat, h_ctx = ffn_pair(h_lat, h_ctx, layer, 6, norm_ffn2, ffn2)

    layer = 1
    h_lat, h_ctx, ffn2_last = ffn_pair(h_lat, h_ctx, layer, 0, norm_ffn1, ffn1, cast_next=(ffn2, layer))

    cos, sin = _rope_tables(seq)
    w_qkv = attn_w_qkv.astype(BF16)
    qkv_lat = _qkv(h_lat, mods, layer, 3, norm_mix, w_qkv, 0, attn_q_gain, attn_k_gain, cos, sin,
                   seq=seq, tm=tm_mm, bidx=bidx_lat(tm_mm), n_q_heads=n_q_heads, n_kv_heads=n_kv_heads,
                   kv_only=False, rope=True)
    no_rope = jnp.zeros((l_len, HEAD_DIM), F32)
    kv_ctx = _qkv(h_ctx, mods, layer, 3, norm_mix, w_qkv, 0, attn_q_gain, attn_k_gain, no_rope, no_rope,
                  seq=l_len, tm=tm_mm_ctx, bidx=bidx_ctx, n_q_heads=n_q_heads, n_kv_heads=n_kv_heads,
                  kv_only=True, rope=False)
    o_lat = _attention(qkv_lat, kv_ctx, n_q_heads, n_kv_heads, tq=min(1024, seq))
    h_lat = _oproj(o_lat.reshape(n_b * seq, d), attn_w_o.astype(BF16), 0, h_lat, mods, layer, 5,
                   tm=tm_mm, bidx=bidx_lat(tm_mm))

    h_lat = _ffn(h_lat, mods, layer, 6, norm_ffn2, *ffn2_last, tm=tm_lat, bidx=bidx_lat(tm_lat))
    return h_lat.reshape(n_b, seq, d)
```

```python
import functools
import math

import jax
import jax.numpy as jnp
from jax import lax
from jax.experimental import pallas as pl
from jax.experimental.pallas import tpu as pltpu

F32 = jnp.float32
BF16 = jnp.bfloat16

EPS = 1e-6
N_MOD = 9
GRID_W = 64
HEAD_DIM = 128
KV_REP = 4
ROPE_THETA = 10000.0
S5_GROUP = 16
S5_STATE = 64

V7X_LANES = 128
V7X_SUBLANES = 8
V7X_VMEM_LIMIT = 60 * 1024 * 1024

S5_GB = 8
S5_TQ = 64
S5_PITCH = 68
COND_ROWS = 16
SIDE_COLS = 8
COL_BLK = 512
NORM_ROWS = 32
ATTN_AHEAD = 1
FFN_OUT_BLK = 512
ATTN_ROWS = 512


def _cparams(sem):
    return pltpu.CompilerParams(dimension_semantics=sem, vmem_limit_bytes=V7X_VMEM_LIMIT)


def _resident(shape, index):
    return pl.BlockSpec(shape, index, pipeline_mode=pl.Buffered(1))


def _ada_kernel(a_ref, w_ref, b_ref, o_ref):
    a = a_ref[...]
    s = (a * jax.nn.sigmoid(a)).astype(BF16)
    o_ref[...] = jnp.dot(s, w_ref[...].astype(BF16), preferred_element_type=F32) + b_ref[...]


def _ada_mods(cond, ada_w, ada_b, tn=1024):
    depth, d, n = ada_w.shape
    rows = cond.shape[0]
    return pl.pallas_call(
        _ada_kernel,
        grid=(depth, n // tn),
        in_specs=[
            pl.BlockSpec((rows, d), lambda l, j: (0, 0)),
            pl.BlockSpec((None, d, tn), lambda l, j: (l, 0, j)),
            pl.BlockSpec((None, 1, tn), lambda l, j: (l, 0, j)),
        ],
        out_specs=pl.BlockSpec((None, rows, tn), lambda l, j: (l, 0, j)),
        out_shape=jax.ShapeDtypeStruct((depth, rows, n), F32),
        compiler_params=_cparams(("arbitrary", "arbitrary")),
        name="ada_mods",
    )(cond, ada_w, ada_b.reshape(depth, 1, n))


def _mod_spec(layer, col, d, bidx):
    return pl.BlockSpec((None, None, 1, d), lambda i, *_: (layer, bidx(i), 0, col))


def _norm_mod(x, g, shift, scale):
    r = lax.rsqrt(jnp.mean(x * x, axis=-1, keepdims=True) + EPS)
    return (x * r * g) * (1.0 + scale) + shift


def _norm_mod_blocks(h_ref, nrm_ref, sh_ref, sc_ref, emit):
    g, shift, scale = nrm_ref[...], sh_ref[...], sc_ref[...]

    def body(i, carry):
        rows = pl.ds(pl.multiple_of(i * NORM_ROWS, NORM_ROWS), NORM_ROWS)
        x = h_ref[rows, :]
        emit(rows, x, _norm_mod(x, g, shift, scale))
        return carry

    lax.fori_loop(0, h_ref.shape[0] // NORM_ROWS, body, 0, unroll=4)


def _ffn_kernel(*refs, emit_w, n_side):
    h_ref, nrm_ref, sh_ref, sc_ref, gt_ref, wg_ref, wu_ref, wd_ref = refs[:8]
    side_in = refs[8:8 + n_side]
    o_ref = refs[8 + n_side]
    w_out = refs[9 + n_side:12 + n_side] if emit_w else ()
    side_out = refs[-1 - n_side:-1]
    m_ref = refs[-1]
    f = pl.program_id(1)

    @pl.when(f == 0)
    def _():
        def emit(rows, x, m):
            m_ref[rows, :] = m.astype(BF16)
            o_ref[rows, :] = x

        _norm_mod_blocks(h_ref, nrm_ref, sh_ref, sc_ref, emit)

    for src, dst in zip(side_in, side_out):
        dst[...] = src[...].astype(BF16)

    m = m_ref[...]
    wg, wu = wg_ref[...].astype(BF16), wu_ref[...].astype(BF16)
    if emit_w:
        w_out[0][...] = wg
        w_out[1][...] = wu
    g = jnp.dot(m, wg, preferred_element_type=F32)
    u = jnp.dot(m, wu, preferred_element_type=F32)
    a = (g * jax.nn.sigmoid(g) * u).astype(BF16)
    for nb in range(o_ref.shape[1] // FFN_OUT_BLK):
        cols = slice(nb * FFN_OUT_BLK, (nb + 1) * FFN_OUT_BLK)
        wd = wd_ref[:, cols].astype(BF16)
        if emit_w:
            w_out[2][:, cols] = wd
        p = jnp.dot(a, wd, preferred_element_type=F32)
        o_ref[:, cols] += (0.5 * gt_ref[:, cols]) * p


def _ffn(h, mods, layer, mod_base, norm_g, w_gate, w_up, w_down, *, tm, bidx, emit_w=False, cast_next=None,
         cast_mats=()):
    n_tok, d = h.shape
    f_dim = w_gate.shape[-1]
    if w_gate.ndim == 3:
        tf = 256
        wspec = lambda shape, idx: pl.BlockSpec((None,) + shape, lambda i, f: (layer,) + idx(f))
    else:
        assert not emit_w and w_gate.dtype == BF16
        tf = 512
        wspec = lambda shape, idx: pl.BlockSpec(shape, lambda i, f: idx(f))
    n_i, n_f = n_tok // tm, f_dim // tf
    out_specs = [pl.BlockSpec((tm, d), lambda i, f: (i, 0))]
    out_shape = [jax.ShapeDtypeStruct((n_tok, d), F32)]
    w_shapes = [jax.ShapeDtypeStruct((d, f_dim), BF16), jax.ShapeDtypeStruct((d, f_dim), BF16),
                jax.ShapeDtypeStruct((f_dim, d), BF16)]
    if emit_w:
        assert n_i == 1, "each weight tile must be visited exactly once"
        out_specs += [pl.BlockSpec((d, tf), lambda i, f: (0, f)), pl.BlockSpec((d, tf), lambda i, f: (0, f)),
                      pl.BlockSpec((tf, d), lambda i, f: (f, 0))]
        out_shape += w_shapes
    side_args, side_specs = [], []
    if cast_next is not None:
        assert not emit_w
        side_args, s_layer = list(cast_next[0]), cast_next[1]
        rb, cb = d // n_i, f_dim // n_f
        side_specs = [pl.BlockSpec((None, rb, cb), lambda i, f: (s_layer, i, f)),
                      pl.BlockSpec((None, rb, cb), lambda i, f: (s_layer, i, f)),
                      pl.BlockSpec((None, cb, rb), lambda i, f: (s_layer, f, i))]
        out_specs += [pl.BlockSpec((rb, cb), lambda i, f: (i, f)), pl.BlockSpec((rb, cb), lambda i, f: (i, f)),
                      pl.BlockSpec((cb, rb), lambda i, f: (f, i))]
        out_shape += w_shapes
    for arr, sub in cast_mats:
        assert not emit_w and n_f >= SIDE_COLS
        rb, cb = arr.shape[1] // n_i, arr.shape[2] // SIDE_COLS
        col = lambda f: jnp.minimum(f, SIDE_COLS - 1)
        side_args.append(arr)
        side_specs.append(pl.BlockSpec((None, rb, cb), lambda i, f, sub=sub: (sub, i, col(f))))
        out_specs.append(pl.BlockSpec((rb, cb), lambda i, f: (i, col(f))))
        out_shape.append(jax.ShapeDtypeStruct(arr.shape[1:], BF16))
    out = pl.pallas_call(
        functools.partial(_ffn_kernel, emit_w=emit_w, n_side=len(side_args)),
        grid=(n_i, n_f),
        in_specs=[
            pl.BlockSpec((tm, d), lambda i, f: (i, 0), pipeline_mode=pl.Buffered(1)),
            pl.BlockSpec((None, 1, d), lambda i, f: (layer, 0, 0)),
            _mod_spec(layer, mod_base + 0, d, bidx),
            _mod_spec(layer, mod_base + 1, d, bidx),
            _mod_spec(layer, mod_base + 2, d, bidx),
            wspec((d, tf), lambda f: (0, f)),
            wspec((d, tf), lambda f: (0, f)),
            wspec((tf, d), lambda f: (f, 0)),
        ] + side_specs,
        out_specs=out_specs,
        out_shape=out_shape,
        scratch_shapes=[pltpu.VMEM((tm, d), BF16)],
        compiler_params=_cparams(("arbitrary", "arbitrary")),
        name="ffn",
    )(h, norm_g.reshape(norm_g.shape[0], 1, d), mods, mods, mods, w_gate, w_up, w_down, *side_args)
    return (out[0], tuple(out[1:])) if len(out) > 1 else out[0]


def _zoh(a_re, a_im, log_dt):
    dt = jnp.exp(log_dt)
    mag = jnp.exp(a_re * dt)
    l_re = mag * jnp.cos(a_im * dt)
    l_im = mag * jnp.sin(a_im * dt)
    den = a_re * a_re + a_im * a_im
    c_re = ((l_re - 1.0) * a_re + l_im * a_im) / den
    c_im = (l_im * a_re - (l_re - 1.0) * a_im) / den
    return l_re, l_im, c_re, c_im


def _lane_tile(x, reps):
    w = x.shape[1]
    row = lax.broadcasted_iota(jnp.int32, (w, w * reps), 0)
    col = lax.broadcasted_iota(jnp.int32, (w, w * reps), 1)
    sel = (row == col % w).astype(BF16)
    return jnp.dot(x, sel, preferred_element_type=F32)


def _s5_param_kernel(are_ref, aim_ref, ldt_ref, bre_ref, bim_ref, cre_ref, cim_ref, ar2_ref, ai2_ref, ld2_ref,
                     obr_ref, obi_ref, ocr_ref, oci_ref, olr_ref, oli_ref):
    gpb = are_ref.shape[0] // S5_GROUP
    _, _, k_re, k_im = _zoh(are_ref[...], aim_ref[...], ldt_ref[...])
    bb_re = k_re * bre_ref[...] - k_im * bim_ref[...]
    bb_im = k_re * bim_ref[...] + k_im * bre_ref[...]
    n_in, n_st = bb_re.shape[0], S5_STATE * gpb
    on_diag = (lax.broadcasted_iota(jnp.int32, (n_in, n_st), 0) // S5_GROUP
               == lax.broadcasted_iota(jnp.int32, (n_in, n_st), 1) // S5_STATE)
    obr_ref[...] = jnp.where(on_diag, _lane_tile(bb_re.astype(BF16), gpb), 0.0).astype(BF16)
    obi_ref[...] = jnp.where(on_diag, _lane_tile(bb_im.astype(BF16), gpb), 0.0).astype(BF16)
    on_diag_t = (lax.broadcasted_iota(jnp.int32, (n_st, n_in), 0) // S5_STATE
                 == lax.broadcasted_iota(jnp.int32, (n_st, n_in), 1) // S5_GROUP)
    ocr_ref[...] = jnp.where(on_diag_t, _lane_tile(cre_ref[...].astype(BF16), gpb), 0.0).astype(BF16)
    oci_ref[...] = jnp.where(on_diag_t, _lane_tile((-cim_ref[...]).astype(BF16), gpb), 0.0).astype(BF16)
    l_re, l_im, _, _ = _zoh(ar2_ref[...], ai2_ref[...], ld2_ref[...])
    olr_ref[...] = l_re
    oli_ref[...] = l_im


def _s5_params(a_re, a_im, log_dt, b_re, b_im, c_re, c_im):
    dirs, g, p = a_re.shape
    hch = b_re.shape[3]
    gpb = g // S5_GB
    rows_in, rows_st = gpb * hch, gpb * p

    def per_channel(v):
        return jnp.broadcast_to(v[:, :, None, :], (dirs, g, hch, p)).reshape(dirs, S5_GB, rows_in, p)

    ldt = jnp.broadcast_to(log_dt[:, :, None], (dirs, g, p))
    args = (
        per_channel(a_re), per_channel(a_im), per_channel(ldt),
        jnp.swapaxes(b_re, 2, 3).reshape(dirs, S5_GB, rows_in, p),
        jnp.swapaxes(b_im, 2, 3).reshape(dirs, S5_GB, rows_in, p),
        jnp.swapaxes(c_re, 2, 3).reshape(dirs, S5_GB, rows_st, hch),
        jnp.swapaxes(c_im, 2, 3).reshape(dirs, S5_GB, rows_st, hch),
        a_re.reshape(dirs, g * p // V7X_LANES, V7X_LANES),
        a_im.reshape(dirs, g * p // V7X_LANES, V7X_LANES),
        ldt.reshape(dirs, g * p // V7X_LANES, V7X_LANES),
    )
    blk = lambda r, c: pl.BlockSpec((None, None, r, c), lambda dd, gb: (dd, gb, 0, 0))
    flat = pl.BlockSpec((None, g * p // V7X_LANES, V7X_LANES), lambda dd, gb: (dd, 0, 0))
    return pl.pallas_call(
        _s5_param_kernel,
        grid=(dirs, S5_GB),
        in_specs=[blk(rows_in, p)] * 5 + [blk(rows_st, hch)] * 2 + [flat] * 3,
        out_specs=[blk(rows_in, rows_st), blk(rows_in, rows_st), blk(rows_st, rows_in), blk(rows_st, rows_in),
                   flat, flat],
        out_shape=[
            jax.ShapeDtypeStruct((dirs, S5_GB, rows_in, rows_st), BF16),
            jax.ShapeDtypeStruct((dirs, S5_GB, rows_in, rows_st), BF16),
            jax.ShapeDtypeStruct((dirs, S5_GB, rows_st, rows_in), BF16),
            jax.ShapeDtypeStruct((dirs, S5_GB, rows_st, rows_in), BF16),
            jax.ShapeDtypeStruct((dirs, g * p // V7X_LANES, V7X_LANES), F32),
            jax.ShapeDtypeStruct((dirs, g * p // V7X_LANES, V7X_LANES), F32),
        ],
        compiler_params=_cparams(("arbitrary", "arbitrary")),
        name="s5_params",
    )(*args)


def _pair_rows(lam):
    dirs, tiles, lanes = lam.shape
    half = V7X_SUBLANES // 2
    x = jnp.broadcast_to(lam.reshape(dirs, 2, tiles // 2, 1, lanes), (dirs, 2, tiles // 2, half, lanes))
    return jnp.transpose(x, (0, 2, 1, 3, 4)).reshape(dirs, tiles // 2, V7X_SUBLANES, lanes)


def _s5_scan_kernel(*refs, n_ctx_chunks, tq, pitch, reverse, ctx_row, emit_z):
    gb_per_half = S5_GB // 2
    n_in = 16 if emit_z else 13
    (hl_ref, hc_ref, nrm_ref, sh_ref, sc_ref, bre_ref, bim_ref, cre_ref, cim_ref, are_ref, aim_ref) = refs[:11]
    if emit_z:
        yl_ref, yc_ref, dsk_ref = refs[11:14]
    ol_ref, oc_ref = refs[n_in - 2:n_in]
    u_st = refs[n_in]
    s_re_k = refs[n_in + 1:n_in + 1 + gb_per_half]
    s_im_k = refs[n_in + 1 + gb_per_half:n_in + 1 + 2 * gb_per_half]
    h_re, h_im = refs[n_in + 1 + 2 * gb_per_half:]
    c = pl.program_id(0)
    n_b = hl_ref.shape[0]
    half_rows = n_b * pitch
    tiles_per_gb = s_re_k[0].shape[0]
    lanes_in = u_st.shape[1] // S5_GB

    @pl.when(c == 0)
    def _():
        h_re[...] = jnp.zeros_like(h_re)
        h_im[...] = jnp.zeros_like(h_im)
        u_st[...] = jnp.zeros_like(u_st)

    is_ctx = c < n_ctx_chunks

    @pl.when(is_ctx)
    def _():
        for b in range(n_b):
            u_st[b * pitch:b * pitch + tq, :] = _norm_mod(hc_ref[b], nrm_ref[...], sh_ref[ctx_row], sc_ref[ctx_row])

    @pl.when(jnp.logical_not(is_ctx))
    def _():
        for b in range(n_b):
            u_st[b * pitch:b * pitch + tq, :] = _norm_mod(hl_ref[b], nrm_ref[...], sh_ref[b], sc_ref[b])

    def project_in(gb):
        lhs = u_st[:, gb * lanes_in:(gb + 1) * lanes_in].astype(BF16)
        p_re = jnp.dot(lhs, bre_ref[gb], preferred_element_type=F32)
        p_im = jnp.dot(lhs, bim_ref[gb], preferred_element_type=F32)
        half = gb // gb_per_half
        rows = slice(half * half_rows, (half + 1) * half_rows)
        s_re, s_im = s_re_k[gb % gb_per_half], s_im_k[gb % gb_per_half]
        for j in range(tiles_per_gb):
            s_re[j, rows, :] = p_re[:, j * V7X_LANES:(j + 1) * V7X_LANES]
            s_im[j, rows, :] = p_im[:, j * V7X_LANES:(j + 1) * V7X_LANES]

    def scan_pass(k):
        s_re, s_im = s_re_k[k], s_im_k[k]
        ns = list(range(k * tiles_per_gb, (k + 1) * tiles_per_gb))
        a_r = [are_ref[n] for n in ns]
        a_i = [aim_ref[n] for n in ns]
        hr = [h_re[n] for n in ns]
        hi = [h_im[n] for n in ns]
        for t in range(tq):
            rows = pl.ds(tq - 1 - t if reverse else t, V7X_SUBLANES, stride=pitch)
            for j in range(tiles_per_gb):
                n_r = a_r[j] * hr[j] - a_i[j] * hi[j] + s_re[j, rows, :]
                n_i = a_r[j] * hi[j] + a_i[j] * hr[j] + s_im[j, rows, :]
                s_re[j, rows, :] = n_r
                s_im[j, rows, :] = n_i
                hr[j], hi[j] = n_r, n_i
        for j, n in enumerate(ns):
            h_re[n] = hr[j]
            h_im[n] = hi[j]

    def project_out(gb):
        half = gb // gb_per_half
        rows = slice(half * half_rows, (half + 1) * half_rows)
        s_re, s_im = s_re_k[gb % gb_per_half], s_im_k[gb % gb_per_half]
        l_re = jnp.concatenate([s_re[j, rows, :] for j in range(tiles_per_gb)], axis=1).astype(BF16)
        l_im = jnp.concatenate([s_im[j, rows, :] for j in range(tiles_per_gb)], axis=1).astype(BF16)
        return (jnp.dot(l_re, cre_ref[gb], preferred_element_type=F32)
                + jnp.dot(l_im, cim_ref[gb], preferred_element_type=F32))

    for gb in range(S5_GB):
        project_in(gb)
    for k in range(gb_per_half):
        scan_pass(k)
    ys = [project_out(gb) for gb in range(S5_GB)]

    def emit(out_ref, yf_ref):
        for gb in range(S5_GB):
            cols = slice(gb * lanes_in, (gb + 1) * lanes_in)
            for b in range(n_b):
                y = ys[gb][b * pitch:b * pitch + tq]
                if emit_z:
                    y = y + yf_ref[b, :, cols] + dsk_ref[:, cols] * u_st[b * pitch:b * pitch + tq, cols]
                    out_ref[b, :, cols] = _gelu_tanh(y).astype(out_ref.dtype)
                else:
                    out_ref[b, :, cols] = y

    @pl.when(is_ctx)
    def _():
        emit(oc_ref, yc_ref if emit_z else None)

    @pl.when(jnp.logical_not(is_ctx))
    def _():
        emit(ol_ref, yl_ref if emit_z else None)


def _gelu_tanh(x):
    return 0.5 * x * (1.0 + jnp.tanh(math.sqrt(2.0 / math.pi) * (x + 0.044715 * (x * x * x))))


def _s5_scan(h_lat, h_ctx, mods, layer, mod_base, norm_g, maps, direction, y_prev=None, d_skip=None):
    n_b, s_len, d = h_lat.shape
    l_len = h_ctx.shape[1]
    tq, pitch = S5_TQ, S5_PITCH
    n_l, n_s = l_len // tq, s_len // tq
    bd_bre, bd_bim, bd_cre, bd_cim, a_re, a_im = maps
    n_pairs = a_re.shape[1]
    reverse = direction == 1
    emit_z = y_prev is not None

    def lat_idx(c):
        k = jnp.maximum(c - n_l, 0)
        return n_s - 1 - k if reverse else k

    def ctx_idx(c):
        k = jnp.minimum(c, n_l - 1)
        return n_l - 1 - k if reverse else k

    def wspec(arr):
        shape = arr.shape[1:]
        return _resident((None,) + shape, lambda c: (direction,) + (0,) * len(shape))

    lat_spec = pl.BlockSpec((n_b, tq, d), lambda c: (0, lat_idx(c), 0))
    ctx_spec = pl.BlockSpec((n_b, tq, d), lambda c: (0, ctx_idx(c), 0))
    mod_rows = lambda col: pl.BlockSpec((None, COND_ROWS, 1, d), lambda c: (layer, 0, 0, col))
    in_specs = [lat_spec, ctx_spec,
                pl.BlockSpec((None, 1, d), lambda c: (layer, 0, 0)),
                mod_rows(mod_base), mod_rows(mod_base + 1),
                wspec(bd_bre), wspec(bd_bim), wspec(bd_cre), wspec(bd_cim), wspec(a_re), wspec(a_im)]
    args = [h_lat, h_ctx, norm_g.reshape(norm_g.shape[0], 1, d), mods, mods,
            bd_bre, bd_bim, bd_cre, bd_cim, a_re, a_im]
    out_dtype = F32
    if emit_z:
        in_specs += [lat_spec, ctx_spec, pl.BlockSpec((None, 1, d), lambda c: (0, 0, 0))]
        args += [y_prev[0], y_prev[1], d_skip.reshape(d_skip.shape[0], 1, d)]
        out_dtype = BF16
    kern = functools.partial(_s5_scan_kernel, n_ctx_chunks=n_l, tq=tq, pitch=pitch, reverse=reverse,
                             ctx_row=n_b, emit_z=emit_z)
    return pl.pallas_call(
        kern,
        grid=(n_l + n_s,),
        in_specs=in_specs,
        out_specs=[lat_spec, ctx_spec],
        out_shape=[jax.ShapeDtypeStruct((n_b, s_len, d), out_dtype),
                   jax.ShapeDtypeStruct((n_b, l_len, d), out_dtype)],
        scratch_shapes=[
            pltpu.VMEM((n_b * pitch, d), F32),
            *[pltpu.VMEM((2 * n_pairs // S5_GB, 2 * n_b * pitch, V7X_LANES), F32) for _ in range(S5_GB)],
            pltpu.VMEM((n_pairs, V7X_SUBLANES, V7X_LANES), F32),
            pltpu.VMEM((n_pairs, V7X_SUBLANES, V7X_LANES), F32),
        ],
        compiler_params=_cparams(("arbitrary",)),
        name="s5_bwd" if reverse else "s5_fwd",
    )(*args)


def _glu_kernel(z_ref, w1_ref, w2_ref, h_ref, gt_ref, o_ref):
    z = z_ref[...]
    for nb in range(o_ref.shape[1] // COL_BLK):
        cols = slice(nb * COL_BLK, (nb + 1) * COL_BLK)
        a = jnp.dot(z, w1_ref[:, cols], preferred_element_type=F32)
        b = jnp.dot(z, w2_ref[:, cols], preferred_element_type=F32)
        o_ref[:, cols] = h_ref[:, cols] + gt_ref[:, cols] * (a * jax.nn.sigmoid(b))


def _glu(z, w1, w2, sub, h, mods, layer, gate_col, *, tm, bidx):
    n_tok, d = h.shape
    return pl.pallas_call(
        _glu_kernel,
        grid=(n_tok // tm,),
        in_specs=[
            pl.BlockSpec((tm, d), lambda i: (i, 0)),
            _resident((None, d, d), lambda i: (sub, 0, 0)),
            _resident((None, d, d), lambda i: (sub, 0, 0)),
            pl.BlockSpec((tm, d), lambda i: (i, 0)),
            _mod_spec(layer, gate_col, d, bidx),
        ],
        out_specs=pl.BlockSpec((tm, d), lambda i: (i, 0)),
        out_shape=jax.ShapeDtypeStruct((n_tok, d), F32),
        compiler_params=_cparams(("arbitrary",)),
        name="s5_glu",
    )(z, w1, w2, h, mods)


def _rope_tables(seq):
    pairs = HEAD_DIM // 4
    freqs = ROPE_THETA ** (-jnp.arange(pairs, dtype=F32) / pairs)
    pos = jnp.arange(seq, dtype=jnp.int32)
    ang_r = (pos // GRID_W).astype(F32)[:, None] * freqs
    ang_c = (pos % GRID_W).astype(F32)[:, None] * freqs
    cos = jnp.concatenate([jnp.cos(ang_r), jnp.cos(ang_r), jnp.cos(ang_c), jnp.cos(ang_c)], axis=-1)
    sin = jnp.concatenate([-jnp.sin(ang_r), jnp.sin(ang_r), -jnp.sin(ang_c), jnp.sin(ang_c)], axis=-1)
    return cos, sin


def _qkv_kernel(h_ref, nrm_ref, sh_ref, sc_ref, w_ref, qg_ref, kg_ref, cos_ref, sin_ref, o_ref,
                *, head0, n_q_heads, n_kv_heads, rope):
    m = _norm_mod(h_ref[...], nrm_ref[...], sh_ref[...], sc_ref[...]).astype(BF16)
    lane = lax.broadcasted_iota(jnp.int32, (1, HEAD_DIM), 1)
    first = (lane % (HEAD_DIM // 2)) < (HEAD_DIM // 4)
    hpb = COL_BLK // HEAD_DIM
    for nb in range(w_ref.shape[1] // COL_BLK):
        acc = jnp.dot(m, w_ref[:, nb * COL_BLK:(nb + 1) * COL_BLK], preferred_element_type=F32)
        for hh in range(hpb):
            head = head0 + nb * hpb + hh
            x = acc[:, hh * HEAD_DIM:(hh + 1) * HEAD_DIM]
            if head < n_q_heads + n_kv_heads:
                gain = qg_ref[...] if head < n_q_heads else kg_ref[...]
                xn = x * lax.rsqrt(jnp.mean(x * x, axis=-1, keepdims=True) + EPS) * gain
                if rope:
                    partner = jnp.where(first, pltpu.roll(xn, HEAD_DIM - HEAD_DIM // 4, 1),
                                        pltpu.roll(xn, HEAD_DIM // 4, 1))
                    xn = xn * cos_ref[...] + partner * sin_ref[...]
                x = xn
            o_ref[nb * hpb + hh] = x.astype(BF16)


def _qkv(h, mods, layer, mod_base, norm_g, w_qkv, sub, q_gain, k_gain, cos, sin, *, seq, tm, bidx,
         n_q_heads, n_kv_heads, kv_only, rope):
    n_tok, d = h.shape
    n_b = n_tok // seq
    n_cols = w_qkv.shape[2]
    head0 = n_q_heads if kv_only else 0
    width = n_cols - head0 * HEAD_DIM
    heads = width // HEAD_DIM
    assert (head0 * HEAD_DIM) % width == 0
    tiles_per_seq = seq // tm
    kern = functools.partial(_qkv_kernel, head0=head0, n_q_heads=n_q_heads, n_kv_heads=n_kv_heads, rope=rope)
    return pl.pallas_call(
        kern,
        grid=(n_tok // tm,),
        in_specs=[
            pl.BlockSpec((tm, d), lambda i: (i, 0)),
            pl.BlockSpec((None, 1, d), lambda i: (layer, 0, 0)),
            _mod_spec(layer, mod_base + 0, d, bidx),
            _mod_spec(layer, mod_base + 1, d, bidx),
            _resident((None, d, width), lambda i: (sub, 0, head0 * HEAD_DIM // width)),
            pl.BlockSpec((None, 1, HEAD_DIM), lambda i: (sub, 0, 0)),
            pl.BlockSpec((None, 1, HEAD_DIM), lambda i: (sub, 0, 0)),
            pl.BlockSpec((tm, HEAD_DIM), lambda i: (i % tiles_per_seq, 0)),
            pl.BlockSpec((tm, HEAD_DIM), lambda i: (i % tiles_per_seq, 0)),
        ],
        out_specs=pl.BlockSpec((None, heads, tm, HEAD_DIM), lambda i: (i // tiles_per_seq, 0, i % tiles_per_seq, 0)),
        out_shape=jax.ShapeDtypeStruct((n_b, heads, seq, HEAD_DIM), BF16),
        compiler_params=_cparams(("arbitrary",)),
        name="qkv",
    )(h, norm_g.reshape(norm_g.shape[0], 1, d), mods, mods, w_qkv,
      q_gain.reshape(q_gain.shape[0], 1, HEAD_DIM), k_gain.reshape(k_gain.shape[0], 1, HEAD_DIM), cos, sin)


def _attn_kernel(q_ref, kl_ref, vl_ref, kc_ref, vc_ref, o_ref):
    rep, tq, hd = q_ref.shape
    dn = (((1,), (1,)), ((), ()))
    c = (HEAD_DIM ** -0.5) * math.log2(math.e)
    blk = min(ATTN_ROWS, tq)
    chains = [(r, slice(sb * blk, (sb + 1) * blk)) for r in range(rep) for sb in range(tq // blk)]

    def scores(chain):
        r, rows = chain
        q = q_ref[r, rows, :]
        return (lax.dot_general(q, kl_ref[...], dn, preferred_element_type=F32),
                lax.dot_general(q, kc_ref[...], dn, preferred_element_type=F32))

    ahead = [scores(ch) for ch in chains[:ATTN_AHEAD]]
    for i, (r, rows) in enumerate(chains):
        s_l, s_c = ahead.pop(0)
        if i + ATTN_AHEAD < len(chains):
            ahead.append(scores(chains[i + ATTN_AHEAD]))
        m = jnp.maximum(jnp.max(s_l, axis=-1, keepdims=True), jnp.max(s_c, axis=-1, keepdims=True))
        p_l = jnp.exp2((s_l - m) * c)
        p_c = jnp.exp2((s_c - m) * c)
        den = jnp.sum(p_l, axis=-1, keepdims=True) + jnp.sum(p_c, axis=-1, keepdims=True)
        o = (jnp.dot(p_l.astype(BF16), vl_ref[...], preferred_element_type=F32)
             + jnp.dot(p_c.astype(BF16), vc_ref[...], preferred_element_type=F32)) / den
        o_ref[rows, r * hd:(r + 1) * hd] = o.astype(BF16)


def _attention(qkv_lat, kv_ctx, n_q_heads, n_kv_heads, *, tq=256):
    n_b, _, seq, hd = qkv_lat.shape
    l_len = kv_ctx.shape[2]
    rep = n_q_heads // n_kv_heads
    return pl.pallas_call(
        _attn_kernel,
        grid=(n_b, n_kv_heads, seq // tq),
        in_specs=[
            pl.BlockSpec((None, rep, tq, hd), lambda b, g, i: (b, g, i, 0)),
            pl.BlockSpec((None, None, seq, hd), lambda b, g, i: (b, n_q_heads + g, 0, 0)),
            pl.BlockSpec((None, None, seq, hd), lambda b, g, i: (b, n_q_heads + n_kv_heads + g, 0, 0)),
            pl.BlockSpec((None, None, l_len, hd), lambda b, g, i: (b, g, 0, 0)),
            pl.BlockSpec((None, None, l_len, hd), lambda b, g, i: (b, n_kv_heads + g, 0, 0)),
        ],
        out_specs=pl.BlockSpec((None, tq, rep * hd), lambda b, g, i: (b, i, g)),
        out_shape=jax.ShapeDtypeStruct((n_b, seq, n_q_heads * hd), BF16),
        compiler_params=_cparams(("arbitrary", "arbitrary", "arbitrary")),
        name="attention",
    )(qkv_lat, qkv_lat, qkv_lat, kv_ctx, kv_ctx)


def _oproj_kernel(x_ref, w_ref, h_ref, gt_ref, o_ref):
    x = x_ref[...]
    for nb in range(o_ref.shape[1] // COL_BLK):
        cols = slice(nb * COL_BLK, (nb + 1) * COL_BLK)
        acc = jnp.dot(x, w_ref[:, cols], preferred_element_type=F32)
        o_ref[:, cols] = h_ref[:, cols] + gt_ref[:, cols] * acc


def _oproj(x, w_o, sub, h, mods, layer, gate_col, *, tm, bidx):
    n_tok, d = h.shape
    k_dim = x.shape[1]
    return pl.pallas_call(
        _oproj_kernel,
        grid=(n_tok // tm,),
        in_specs=[
            pl.BlockSpec((tm, k_dim), lambda i: (i, 0)),
            _resident((None, k_dim, d), lambda i: (sub, 0, 0)),
            pl.BlockSpec((tm, d), lambda i: (i, 0)),
            _mod_spec(layer, gate_col, d, bidx),
        ],
        out_specs=pl.BlockSpec((tm, d), lambda i: (i, 0)),
        out_shape=jax.ShapeDtypeStruct((n_tok, d), F32),
        compiler_params=_cparams(("arbitrary",)),
        name="oproj",
    )(x, w_o, h, mods)


def kernel(x, c, ctx, c_ctx, ada_w, ada_b, norm_ffn1, norm_mix, norm_ffn2, ffn1_w_gate, ffn1_w_up, ffn1_w_down, ffn2_w_gate, ffn2_w_up, ffn2_w_down, s5_a_re, s5_a_im, s5_log_dt, s5_b_re, s5_b_im, s5_c_re, s5_c_im, s5_d, s5_glu_w1, s5_glu_w2, attn_w_qkv, attn_w_o, attn_q_gain, attn_k_gain):
    n_b, seq, d = x.shape
    l_len = ctx.shape[1]
    depth = ada_w.shape[0]
    assert depth == 2 and ada_w.shape[2] == N_MOD * d
    assert n_b == 4, "the S5 scan packs 4 sequences x 2 lane tiles onto the 8 sublanes"
    n_q_heads = d // HEAD_DIM
    n_kv_heads = n_q_heads // KV_REP

    tm_lat = min(1024, seq)
    tm_ctx = min(1024, n_b * l_len)
    tm_mm = min(512, seq)
    tm_qkv = min(256, seq)
    tm_mm_ctx = min(512, l_len)
    ctx_row = n_b
    bidx_lat = lambda tm: (lambda i: (i * tm) // seq)
    bidx_ctx = lambda i: ctx_row

    cond = jnp.concatenate([c, c_ctx[None, :], jnp.zeros((COND_ROWS - n_b - 1, d), F32)], axis=0)
    mods = _ada_mods(cond, ada_w, ada_b).reshape(depth, COND_ROWS, 1, N_MOD * d)

    ffn1 = (ffn1_w_gate, ffn1_w_up, ffn1_w_down)
    ffn2 = (ffn2_w_gate, ffn2_w_up, ffn2_w_down)

    h_lat = x.reshape(n_b * seq, d)
    h_ctx = ctx.reshape(n_b * l_len, d)

    def ffn_pair(h_lat, h_ctx, layer, mod_base, norm_g, weights, cast_next=None, cast_mats=()):
        h_ctx, w_bf16 = _ffn(h_ctx, mods, layer, mod_base, norm_g, *weights, tm=tm_ctx, bidx=bidx_ctx, emit_w=True)
        h_lat, cast = _ffn(h_lat, mods, layer, mod_base, norm_g, *w_bf16, tm=tm_lat, bidx=bidx_lat(tm_lat),
                           cast_next=cast_next, cast_mats=cast_mats)
        return h_lat, h_ctx, cast

    layer = 0
    h_lat, h_ctx, (glu_w1, glu_w2) = ffn_pair(h_lat, h_ctx, layer, 0, norm_ffn1, ffn1,
                                              cast_mats=((s5_glu_w1, 0), (s5_glu_w2, 0)))
    glu_w1, glu_w2 = glu_w1[None], glu_w2[None]

    params = _s5_params(s5_a_re[0], s5_a_im[0], s5_log_dt[0], s5_b_re[0], s5_b_im[0], s5_c_re[0], s5_c_im[0])
    maps = tuple(params[:4]) + (_pair_rows(params[4]), _pair_rows(params[5]))
    h3_lat, h3_ctx = h_lat.reshape(n_b, seq, d), h_ctx.reshape(n_b, l_len, d)
    y_fwd = _s5_scan(h3_lat, h3_ctx, mods, layer, 3, norm_mix, maps, 0)
    z_lat, z_ctx = _s5_scan(h3_lat, h3_ctx, mods, layer, 3, norm_mix, maps, 1, y_prev=y_fwd, d_skip=s5_d)
    h_lat = _glu(z_lat.reshape(n_b * seq, d), glu_w1, glu_w2, 0, h_lat, mods, layer, 5,
                 tm=tm_mm, bidx=bidx_lat(tm_mm))
    h_ctx = _glu(z_ctx.reshape(n_b * l_len, d), glu_w1, glu_w2, 0, h_ctx, mods, layer, 5,
                 tm=tm_mm_ctx, bidx=bidx_ctx)

    h_lat, h_ctx, (w_qkv, w_o) = ffn_pair(h_lat, h_ctx, layer, 6, norm_ffn2, ffn2,
                                          cast_mats=((attn_w_qkv, 0), (attn_w_o, 0)))
    w_qkv, w_o = w_qkv[None], w_o[None]

    layer = 1
    h_lat, h_ctx, ffn2_last = ffn_pair(h_lat, h_ctx, layer, 0, norm_ffn1, ffn1, cast_next=(ffn2, layer))

    cos, sin = _rope_tables(seq)
    qkv_lat = _qkv(h_lat, mods, layer, 3, norm_mix, w_qkv, 0, attn_q_gain, attn_k_gain, cos, sin,
                   seq=seq, tm=tm_qkv, bidx=bidx_lat(tm_qkv), n_q_heads=n_q_heads, n_kv_heads=n_kv_heads,
                   kv_only=False, rope=True)
    no_rope = jnp.zeros((l_len, HEAD_DIM), F32)
    kv_ctx = _qkv(h_ctx, mods, layer, 3, norm_mix, w_qkv, 0, attn_q_gain, attn_k_gain, no_rope, no_rope,
                  seq=l_len, tm=tm_mm_ctx, bidx=bidx_ctx, n_q_heads=n_q_heads, n_kv_heads=n_kv_heads,
                  kv_only=True, rope=False)
    o_lat = _attention(qkv_lat, kv_ctx, n_q_heads, n_kv_heads, tq=min(1024, seq))
    h_lat = _oproj(o_lat.reshape(n_b * seq, d), w_o, 0, h_lat, mods, layer, 5,
                   tm=tm_mm, bidx=bidx_lat(tm_mm))

    h_lat = _ffn(h_lat, mods, layer, 6, norm_ffn2, *ffn2_last, tm=tm_lat, bidx=bidx_lat(tm_lat))
    return h_lat.reshape(n_b, seq, d)
```

```python
import functools
import math

import jax
import jax.numpy as jnp
from jax import lax
from jax.experimental import pallas as pl
from jax.experimental.pallas import tpu as pltpu

F32 = jnp.float32
BF16 = jnp.bfloat16

EPS = 1e-6
N_MOD = 9
GRID_W = 64
HEAD_DIM = 128
KV_REP = 4
ROPE_THETA = 10000.0
S5_GROUP = 16
S5_STATE = 64

V7X_LANES = 128
V7X_SUBLANES = 8
V7X_VMEM_LIMIT = 60 * 1024 * 1024

S5_GB = 8
S5_TQ = 64
S5_PITCH = 68
COND_ROWS = 16
SIDE_COLS = 8
COL_BLK = 512
NORM_ROWS = 32
ATTN_AHEAD = 1
FFN_OUT_BLK = 512
ATTN_ROWS = 512


def _cparams(sem):
    return pltpu.CompilerParams(dimension_semantics=sem, vmem_limit_bytes=V7X_VMEM_LIMIT)


def _resident(shape, index):
    return pl.BlockSpec(shape, index, pipeline_mode=pl.Buffered(1))


def _ada_kernel(a_ref, w_ref, b_ref, o_ref):
    a = a_ref[...]
    s = (a * jax.nn.sigmoid(a)).astype(BF16)
    o_ref[...] = jnp.dot(s, w_ref[...].astype(BF16), preferred_element_type=F32) + b_ref[...]


def _ada_mods(cond, ada_w, ada_b, tn=1024):
    depth, d, n = ada_w.shape
    rows = cond.shape[0]
    return pl.pallas_call(
        _ada_kernel,
        grid=(depth, n // tn),
        in_specs=[
            pl.BlockSpec((rows, d), lambda l, j: (0, 0)),
            pl.BlockSpec((None, d, tn), lambda l, j: (l, 0, j)),
            pl.BlockSpec((None, 1, tn), lambda l, j: (l, 0, j)),
        ],
        out_specs=pl.BlockSpec((None, rows, tn), lambda l, j: (l, 0, j)),
        out_shape=jax.ShapeDtypeStruct((depth, rows, n), F32),
        compiler_params=_cparams(("arbitrary", "arbitrary")),
        name="ada_mods",
    )(cond, ada_w, ada_b.reshape(depth, 1, n))


def _mod_spec(layer, col, d, bidx):
    return pl.BlockSpec((None, None, 1, d), lambda i, *_: (layer, bidx(i), 0, col))


def _norm_mod(x, g, shift, scale):
    r = lax.rsqrt(jnp.mean(x * x, axis=-1, keepdims=True) + EPS)
    return (x * r * g) * (1.0 + scale) + shift


def _norm_mod_blocks(h_ref, nrm_ref, sh_ref, sc_ref, emit):
    g, shift, scale = nrm_ref[...], sh_ref[...], sc_ref[...]

    def body(i, carry):
        rows = pl.ds(pl.multiple_of(i * NORM_ROWS, NORM_ROWS), NORM_ROWS)
        x = h_ref[rows, :]
        emit(rows, x, _norm_mod(x, g, shift, scale))
        return carry

    lax.fori_loop(0, h_ref.shape[0] // NORM_ROWS, body, 0, unroll=4)


def _ffn_kernel(*refs, emit_w, n_side):
    h_ref, nrm_ref, sh_ref, sc_ref, gt_ref, wg_ref, wu_ref, wd_ref = refs[:8]
    side_in = refs[8:8 + n_side]
    o_ref = refs[8 + n_side]
    w_out = refs[9 + n_side:12 + n_side] if emit_w else ()
    side_out = refs[-1 - n_side:-1]
    m_ref = refs[-1]
    f = pl.program_id(1)

    @pl.when(f == 0)
    def _():
        def emit(rows, x, m):
            m_ref[rows, :] = m.astype(BF16)
            o_ref[rows, :] = x

        _norm_mod_blocks(h_ref, nrm_ref, sh_ref, sc_ref, emit)

    for src, dst in zip(side_in, side_out):
        dst[...] = src[...].astype(BF16)

    m = m_ref[...]
    wg, wu = wg_ref[...].astype(BF16), wu_ref[...].astype(BF16)
    if emit_w:
        w_out[0][...] = wg
        w_out[1][...] = wu
    g = jnp.dot(m, wg, preferred_element_type=F32)
    u = jnp.dot(m, wu, preferred_element_type=F32)
    a = (g * jax.nn.sigmoid(g) * u).astype(BF16)
    for nb in range(o_ref.shape[1] // FFN_OUT_BLK):
        cols = slice(nb * FFN_OUT_BLK, (nb + 1) * FFN_OUT_BLK)
        wd = wd_ref[:, cols].astype(BF16)
        if emit_w:
            w_out[2][:, cols] = wd
        p = jnp.dot(a, wd, preferred_element_type=F32)
        o_ref[:, cols] += (0.5 * gt_ref[:, cols]) * p


def _ffn(h, mods, layer, mod_base, norm_g, w_gate, w_up, w_down, *, tm, bidx, emit_w=False, cast_next=None,
         cast_mats=()):
    n_tok, d = h.shape
    f_dim = w_gate.shape[-1]
    if w_gate.ndim == 3:
        tf = 256
        wspec = lambda shape, idx: pl.BlockSpec((None,) + shape, lambda i, f: (layer,) + idx(f))
    else:
        assert not emit_w and w_gate.dtype == BF16
        tf = 512
        wspec = lambda shape, idx: pl.BlockSpec(shape, lambda i, f: idx(f))
    n_i, n_f = n_tok // tm, f_dim // tf
    out_specs = [pl.BlockSpec((tm, d), lambda i, f: (i, 0))]
    out_shape = [jax.ShapeDtypeStruct((n_tok, d), F32)]
    w_shapes = [jax.ShapeDtypeStruct((d, f_dim), BF16), jax.ShapeDtypeStruct((d, f_dim), BF16),
                jax.ShapeDtypeStruct((f_dim, d), BF16)]
    if emit_w:
        assert n_i == 1, "each weight tile must be visited exactly once"
        out_specs += [pl.BlockSpec((d, tf), lambda i, f: (0, f)), pl.BlockSpec((d, tf), lambda i, f: (0, f)),
                      pl.BlockSpec((tf, d), lambda i, f: (f, 0))]
        out_shape += w_shapes
    side_args, side_specs = [], []
    if cast_next is not None:
        assert not emit_w
        side_args, s_layer = list(cast_next[0]), cast_next[1]
        rb, cb = d // n_i, f_dim // n_f
        side_specs = [pl.BlockSpec((None, rb, cb), lambda i, f: (s_layer, i, f)),
                      pl.BlockSpec((None, rb, cb), lambda i, f: (s_layer, i, f)),
                      pl.BlockSpec((None, cb, rb), lambda i, f: (s_layer, f, i))]
        out_specs += [pl.BlockSpec((rb, cb), lambda i, f: (i, f)), pl.BlockSpec((rb, cb), lambda i, f: (i, f)),
                      pl.BlockSpec((cb, rb), lambda i, f: (f, i))]
        out_shape += w_shapes
    for arr, sub in cast_mats:
        assert not emit_w and n_f >= SIDE_COLS
        rb, cb = arr.shape[1] // n_i, arr.shape[2] // SIDE_COLS
        col = lambda f: jnp.minimum(f, SIDE_COLS - 1)
        side_args.append(arr)
        side_specs.append(pl.BlockSpec((None, rb, cb), lambda i, f, sub=sub: (sub, i, col(f))))
        out_specs.append(pl.BlockSpec((rb, cb), lambda i, f: (i, col(f))))
        out_shape.append(jax.ShapeDtypeStruct(arr.shape[1:], BF16))
    out = pl.pallas_call(
        functools.partial(_ffn_kernel, emit_w=emit_w, n_side=len(side_args)),
        grid=(n_i, n_f),
        in_specs=[
            pl.BlockSpec((tm, d), lambda i, f: (i, 0), pipeline_mode=pl.Buffered(1)),
            pl.BlockSpec((None, 1, d), lambda i, f: (layer, 0, 0)),
            _mod_spec(layer, mod_base + 0, d, bidx),
            _mod_spec(layer, mod_base + 1, d, bidx),
            _mod_spec(layer, mod_base + 2, d, bidx),
            wspec((d, tf), lambda f: (0, f)),
            wspec((d, tf), lambda f: (0, f)),
            wspec((tf, d), lambda f: (f, 0)),
        ] + side_specs,
        out_specs=out_specs,
        out_shape=out_shape,
        scratch_shapes=[pltpu.VMEM((tm, d), BF16)],
        compiler_params=_cparams(("arbitrary", "arbitrary")),
        name="ffn",
    )(h, norm_g.reshape(norm_g.shape[0], 1, d), mods, mods, mods, w_gate, w_up, w_down, *side_args)
    return (out[0], tuple(out[1:])) if len(out) > 1 else out[0]


def _zoh(a_re, a_im, log_dt):
    dt = jnp.exp(log_dt)
    mag = jnp.exp(a_re * dt)
    l_re = mag * jnp.cos(a_im * dt)
    l_im = mag * jnp.sin(a_im * dt)
    den = a_re * a_re + a_im * a_im
    c_re = ((l_re - 1.0) * a_re + l_im * a_im) / den
    c_im = (l_im * a_re - (l_re - 1.0) * a_im) / den
    return l_re, l_im, c_re, c_im


def _lane_tile(x, reps):
    w = x.shape[1]
    row = lax.broadcasted_iota(jnp.int32, (w, w * reps), 0)
    col = lax.broadcasted_iota(jnp.int32, (w, w * reps), 1)
    sel = (row == col % w).astype(BF16)
    return jnp.dot(x, sel, preferred_element_type=F32)


def _s5_param_kernel(are_ref, aim_ref, ldt_ref, bre_ref, bim_ref, cre_ref, cim_ref, ar2_ref, ai2_ref, ld2_ref,
                     obr_ref, obi_ref, ocr_ref, oci_ref, olr_ref, oli_ref):
    gpb, _, n_state = are_ref.shape
    _, _, k_re, k_im = _zoh(are_ref[...], aim_ref[...], ldt_ref[...])
    rows = (gpb * S5_GROUP, n_state)
    k_re = jnp.broadcast_to(k_re, (gpb, S5_GROUP, n_state)).reshape(rows)
    k_im = jnp.broadcast_to(k_im, (gpb, S5_GROUP, n_state)).reshape(rows)
    bb_re = k_re * bre_ref[...] - k_im * bim_ref[...]
    bb_im = k_re * bim_ref[...] + k_im * bre_ref[...]
    n_in, n_st = bb_re.shape[0], S5_STATE * gpb
    on_diag = (lax.broadcasted_iota(jnp.int32, (n_in, n_st), 0) // S5_GROUP
               == lax.broadcasted_iota(jnp.int32, (n_in, n_st), 1) // S5_STATE)
    obr_ref[...] = jnp.where(on_diag, _lane_tile(bb_re.astype(BF16), gpb), 0.0).astype(BF16)
    obi_ref[...] = jnp.where(on_diag, _lane_tile(bb_im.astype(BF16), gpb), 0.0).astype(BF16)
    on_diag_t = (lax.broadcasted_iota(jnp.int32, (n_st, n_in), 0) // S5_STATE
                 == lax.broadcasted_iota(jnp.int32, (n_st, n_in), 1) // S5_GROUP)
    ocr_ref[...] = jnp.where(on_diag_t, _lane_tile(cre_ref[...].astype(BF16), gpb), 0.0).astype(BF16)
    oci_ref[...] = jnp.where(on_diag_t, _lane_tile((-cim_ref[...]).astype(BF16), gpb), 0.0).astype(BF16)
    l_re, l_im, _, _ = _zoh(ar2_ref[...], ai2_ref[...], ld2_ref[...])
    olr_ref[...] = l_re
    oli_ref[...] = l_im


def _s5_params(a_re, a_im, log_dt, b_re, b_im, c_re, c_im):
    dirs, g, p = a_re.shape
    hch = b_re.shape[3]
    gpb = g // S5_GB
    rows_in, rows_st = gpb * hch, gpb * p

    per_group = lambda v: v.reshape(dirs, S5_GB, gpb, 1, p)
    ldt = jnp.broadcast_to(log_dt[:, :, None], (dirs, g, p))
    args = (
        per_group(a_re), per_group(a_im), per_group(ldt),
        jnp.swapaxes(b_re, 2, 3).reshape(dirs, S5_GB, rows_in, p),
        jnp.swapaxes(b_im, 2, 3).reshape(dirs, S5_GB, rows_in, p),
        jnp.swapaxes(c_re, 2, 3).reshape(dirs, S5_GB, rows_st, hch),
        jnp.swapaxes(c_im, 2, 3).reshape(dirs, S5_GB, rows_st, hch),
        a_re.reshape(dirs, g * p // V7X_LANES, V7X_LANES),
        a_im.reshape(dirs, g * p // V7X_LANES, V7X_LANES),
        ldt.reshape(dirs, g * p // V7X_LANES, V7X_LANES),
    )
    blk = lambda r, c: pl.BlockSpec((None, None, r, c), lambda dd, gb: (dd, gb, 0, 0))
    flat = pl.BlockSpec((None, g * p // V7X_LANES, V7X_LANES), lambda dd, gb: (dd, 0, 0))
    return pl.pallas_call(
        _s5_param_kernel,
        grid=(dirs, S5_GB),
        in_specs=[pl.BlockSpec((None, None, gpb, 1, p), lambda dd, gb: (dd, gb, 0, 0, 0))] * 3
        + [blk(rows_in, p)] * 2 + [blk(rows_st, hch)] * 2 + [flat] * 3,
        out_specs=[blk(rows_in, rows_st), blk(rows_in, rows_st), blk(rows_st, rows_in), blk(rows_st, rows_in),
                   flat, flat],
        out_shape=[
            jax.ShapeDtypeStruct((dirs, S5_GB, rows_in, rows_st), BF16),
            jax.ShapeDtypeStruct((dirs, S5_GB, rows_in, rows_st), BF16),
            jax.ShapeDtypeStruct((dirs, S5_GB, rows_st, rows_in), BF16),
            jax.ShapeDtypeStruct((dirs, S5_GB, rows_st, rows_in), BF16),
            jax.ShapeDtypeStruct((dirs, g * p // V7X_LANES, V7X_LANES), F32),
            jax.ShapeDtypeStruct((dirs, g * p // V7X_LANES, V7X_LANES), F32),
        ],
        compiler_params=_cparams(("arbitrary", "arbitrary")),
        name="s5_params",
    )(*args)


def _pair_rows(lam):
    dirs, tiles, lanes = lam.shape
    half = V7X_SUBLANES // 2
    x = jnp.broadcast_to(lam.reshape(dirs, 2, tiles // 2, 1, lanes), (dirs, 2, tiles // 2, half, lanes))
    return jnp.transpose(x, (0, 2, 1, 3, 4)).reshape(dirs, tiles // 2, V7X_SUBLANES, lanes)


def _s5_scan_kernel(*refs, n_ctx_chunks, tq, pitch, reverse, ctx_row, emit_z):
    gb_per_half = S5_GB // 2
    n_in = 16 if emit_z else 13
    (hl_ref, hc_ref, nrm_ref, sh_ref, sc_ref, bre_ref, bim_ref, cre_ref, cim_ref, are_ref, aim_ref) = refs[:11]
    if emit_z:
        yl_ref, yc_ref, dsk_ref = refs[11:14]
    ol_ref, oc_ref = refs[n_in - 2:n_in]
    u_st = refs[n_in]
    s_re_k = refs[n_in + 1:n_in + 1 + gb_per_half]
    s_im_k = refs[n_in + 1 + gb_per_half:n_in + 1 + 2 * gb_per_half]
    h_re, h_im = refs[n_in + 1 + 2 * gb_per_half:]
    c = pl.program_id(0)
    n_b = hl_ref.shape[0]
    half_rows = n_b * pitch
    tiles_per_gb = s_re_k[0].shape[0]
    lanes_in = u_st.shape[1] // S5_GB

    @pl.when(c == 0)
    def _():
        h_re[...] = jnp.zeros_like(h_re)
        h_im[...] = jnp.zeros_like(h_im)
        u_st[...] = jnp.zeros_like(u_st)

    is_ctx = c < n_ctx_chunks

    @pl.when(is_ctx)
    def _():
        for b in range(n_b):
            u_st[b * pitch:b * pitch + tq, :] = _norm_mod(hc_ref[b], nrm_ref[...], sh_ref[ctx_row], sc_ref[ctx_row])

    @pl.when(jnp.logical_not(is_ctx))
    def _():
        for b in range(n_b):
            u_st[b * pitch:b * pitch + tq, :] = _norm_mod(hl_ref[b], nrm_ref[...], sh_ref[b], sc_ref[b])

    def project_in(gb):
        lhs = u_st[:, gb * lanes_in:(gb + 1) * lanes_in].astype(BF16)
        p_re = jnp.dot(lhs, bre_ref[gb], preferred_element_type=F32)
        p_im = jnp.dot(lhs, bim_ref[gb], preferred_element_type=F32)
        half = gb // gb_per_half
        rows = slice(half * half_rows, (half + 1) * half_rows)
        s_re, s_im = s_re_k[gb % gb_per_half], s_im_k[gb % gb_per_half]
        for j in range(tiles_per_gb):
            s_re[j, rows, :] = p_re[:, j * V7X_LANES:(j + 1) * V7X_LANES]
            s_im[j, rows, :] = p_im[:, j * V7X_LANES:(j + 1) * V7X_LANES]

    def scan_pass(k):
        s_re, s_im = s_re_k[k], s_im_k[k]
        ns = list(range(k * tiles_per_gb, (k + 1) * tiles_per_gb))
        a_r = [are_ref[n] for n in ns]
        a_i = [aim_ref[n] for n in ns]
        hr = [h_re[n] for n in ns]
        hi = [h_im[n] for n in ns]
        for t in range(tq):
            rows = pl.ds(tq - 1 - t if reverse else t, V7X_SUBLANES, stride=pitch)
            for j in range(tiles_per_gb):
                n_r = a_r[j] * hr[j] - a_i[j] * hi[j] + s_re[j, rows, :]
                n_i = a_r[j] * hi[j] + a_i[j] * hr[j] + s_im[j, rows, :]
                s_re[j, rows, :] = n_r
                s_im[j, rows, :] = n_i
                hr[j], hi[j] = n_r, n_i
        for j, n in enumerate(ns):
            h_re[n] = hr[j]
            h_im[n] = hi[j]

    def project_out(gb):
        half = gb // gb_per_half
        rows = slice(half * half_rows, (half + 1) * half_rows)
        s_re, s_im = s_re_k[gb % gb_per_half], s_im_k[gb % gb_per_half]
        l_re = jnp.concatenate([s_re[j, rows, :] for j in range(tiles_per_gb)], axis=1).astype(BF16)
        l_im = jnp.concatenate([s_im[j, rows, :] for j in range(tiles_per_gb)], axis=1).astype(BF16)
        return (jnp.dot(l_re, cre_ref[gb], preferred_element_type=F32)
                + jnp.dot(l_im, cim_ref[gb], preferred_element_type=F32))

    for gb in range(S5_GB):
        project_in(gb)
    for k in range(gb_per_half):
        scan_pass(k)
    ys = [project_out(gb) for gb in range(S5_GB)]

    def emit(out_ref, yf_ref):
        for gb in range(S5_GB):
            cols = slice(gb * lanes_in, (gb + 1) * lanes_in)
            for b in range(n_b):
                y = ys[gb][b * pitch:b * pitch + tq]
                if emit_z:
                    y = y + yf_ref[b, :, cols] + dsk_ref[:, cols] * u_st[b * pitch:b * pitch + tq, cols]
                    out_ref[b, :, cols] = _gelu_tanh(y).astype(out_ref.dtype)
                else:
                    out_ref[b, :, cols] = y

    @pl.when(is_ctx)
    def _():
        emit(oc_ref, yc_ref if emit_z else None)

    @pl.when(jnp.logical_not(is_ctx))
    def _():
        emit(ol_ref, yl_ref if emit_z else None)


def _gelu_tanh(x):
    return 0.5 * x * (1.0 + jnp.tanh(math.sqrt(2.0 / math.pi) * (x + 0.044715 * (x * x * x))))


def _s5_scan(h_lat, h_ctx, mods, layer, mod_base, norm_g, maps, direction, y_prev=None, d_skip=None):
    n_b, s_len, d = h_lat.shape
    l_len = h_ctx.shape[1]
    tq, pitch = S5_TQ, S5_PITCH
    n_l, n_s = l_len // tq, s_len // tq
    bd_bre, bd_bim, bd_cre, bd_cim, a_re, a_im = maps
    n_pairs = a_re.shape[1]
    reverse = direction == 1
    emit_z = y_prev is not None

    def lat_idx(c):
        k = jnp.maximum(c - n_l, 0)
        return n_s - 1 - k if reverse else k

    def ctx_idx(c):
        k = jnp.minimum(c, n_l - 1)
        return n_l - 1 - k if reverse else k

    def wspec(arr):
        shape = arr.shape[1:]
        return _resident((None,) + shape, lambda c: (direction,) + (0,) * len(shape))

    lat_spec = pl.BlockSpec((n_b, tq, d), lambda c: (0, lat_idx(c), 0))
    ctx_spec = pl.BlockSpec((n_b, tq, d), lambda c: (0, ctx_idx(c), 0))
    mod_rows = lambda col: pl.BlockSpec((None, COND_ROWS, 1, d), lambda c: (layer, 0, 0, col))
    in_specs = [lat_spec, ctx_spec,
                pl.BlockSpec((None, 1, d), lambda c: (layer, 0, 0)),
                mod_rows(mod_base), mod_rows(mod_base + 1),
                wspec(bd_bre), wspec(bd_bim), wspec(bd_cre), wspec(bd_cim), wspec(a_re), wspec(a_im)]
    args = [h_lat, h_ctx, norm_g.reshape(norm_g.shape[0], 1, d), mods, mods,
            bd_bre, bd_bim, bd_cre, bd_cim, a_re, a_im]
    out_dtype = F32
    if emit_z:
        in_specs += [lat_spec, ctx_spec, pl.BlockSpec((None, 1, d), lambda c: (0, 0, 0))]
        args += [y_prev[0], y_prev[1], d_skip.reshape(d_skip.shape[0], 1, d)]
        out_dtype = BF16
    kern = functools.partial(_s5_scan_kernel, n_ctx_chunks=n_l, tq=tq, pitch=pitch, reverse=reverse,
                             ctx_row=n_b, emit_z=emit_z)
    return pl.pallas_call(
        kern,
        grid=(n_l + n_s,),
        in_specs=in_specs,
        out_specs=[lat_spec, ctx_spec],
        out_shape=[jax.ShapeDtypeStruct((n_b, s_len, d), out_dtype),
                   jax.ShapeDtypeStruct((n_b, l_len, d), out_dtype)],
        scratch_shapes=[
            pltpu.VMEM((n_b * pitch, d), F32),
            *[pltpu.VMEM((2 * n_pairs // S5_GB, 2 * n_b * pitch, V7X_LANES), F32) for _ in range(S5_GB)],
            pltpu.VMEM((n_pairs, V7X_SUBLANES, V7X_LANES), F32),
            pltpu.VMEM((n_pairs, V7X_SUBLANES, V7X_LANES), F32),
        ],
        compiler_params=_cparams(("arbitrary",)),
        name="s5_bwd" if reverse else "s5_fwd",
    )(*args)


def _glu_kernel(z_ref, w1_ref, w2_ref, h_ref, gt_ref, o_ref):
    z = z_ref[...]
    for nb in range(o_ref.shape[1] // COL_BLK):
        cols = slice(nb * COL_BLK, (nb + 1) * COL_BLK)
        a = jnp.dot(z, w1_ref[:, cols], preferred_element_type=F32)
        b = jnp.dot(z, w2_ref[:, cols], preferred_element_type=F32)
        o_ref[:, cols] = h_ref[:, cols] + gt_ref[:, cols] * (a * jax.nn.sigmoid(b))


def _glu(z, w1, w2, sub, h, mods, layer, gate_col, *, tm, bidx):
    n_tok, d = h.shape
    return pl.pallas_call(
        _glu_kernel,
        grid=(n_tok // tm,),
        in_specs=[
            pl.BlockSpec((tm, d), lambda i: (i, 0)),
            _resident((None, d, d), lambda i: (sub, 0, 0)),
            _resident((None, d, d), lambda i: (sub, 0, 0)),
            pl.BlockSpec((tm, d), lambda i: (i, 0)),
            _mod_spec(layer, gate_col, d, bidx),
        ],
        out_specs=pl.BlockSpec((tm, d), lambda i: (i, 0)),
        out_shape=jax.ShapeDtypeStruct((n_tok, d), F32),
        compiler_params=_cparams(("arbitrary",)),
        name="s5_glu",
    )(z, w1, w2, h, mods)


def _rope_tables(seq):
    pairs = HEAD_DIM // 4
    freqs = ROPE_THETA ** (-jnp.arange(pairs, dtype=F32) / pairs)
    pos = jnp.arange(seq, dtype=jnp.int32)
    ang_r = (pos // GRID_W).astype(F32)[:, None] * freqs
    ang_c = (pos % GRID_W).astype(F32)[:, None] * freqs
    cos = jnp.concatenate([jnp.cos(ang_r), jnp.cos(ang_r), jnp.cos(ang_c), jnp.cos(ang_c)], axis=-1)
    sin = jnp.concatenate([-jnp.sin(ang_r), jnp.sin(ang_r), -jnp.sin(ang_c), jnp.sin(ang_c)], axis=-1)
    return cos, sin


def _qkv_kernel(h_ref, nrm_ref, sh_ref, sc_ref, w_ref, qg_ref, kg_ref, cos_ref, sin_ref, o_ref,
                *, head0, n_q_heads, n_kv_heads, rope):
    m = _norm_mod(h_ref[...], nrm_ref[...], sh_ref[...], sc_ref[...]).astype(BF16)
    lane = lax.broadcasted_iota(jnp.int32, (1, HEAD_DIM), 1)
    first = (lane % (HEAD_DIM // 2)) < (HEAD_DIM // 4)
    hpb = COL_BLK // HEAD_DIM
    for nb in range(w_ref.shape[1] // COL_BLK):
        acc = jnp.dot(m, w_ref[:, nb * COL_BLK:(nb + 1) * COL_BLK], preferred_element_type=F32)
        for hh in range(hpb):
            head = head0 + nb * hpb + hh
            x = acc[:, hh * HEAD_DIM:(hh + 1) * HEAD_DIM]
            if head < n_q_heads + n_kv_heads:
                gain = qg_ref[...] if head < n_q_heads else kg_ref[...]
                xn = x * lax.rsqrt(jnp.mean(x * x, axis=-1, keepdims=True) + EPS) * gain
                if rope:
                    partner = jnp.where(first, pltpu.roll(xn, HEAD_DIM - HEAD_DIM // 4, 1),
                                        pltpu.roll(xn, HEAD_DIM // 4, 1))
                    xn = xn * cos_ref[...] + partner * sin_ref[...]
                x = xn
            o_ref[nb * hpb + hh] = x.astype(BF16)


def _qkv(h, mods, layer, mod_base, norm_g, w_qkv, sub, q_gain, k_gain, cos, sin, *, seq, tm, bidx,
         n_q_heads, n_kv_heads, kv_only, rope):
    n_tok, d = h.shape
    n_b = n_tok // seq
    n_cols = w_qkv.shape[2]
    head0 = n_q_heads if kv_only else 0
    width = n_cols - head0 * HEAD_DIM
    heads = width // HEAD_DIM
    assert (head0 * HEAD_DIM) % width == 0
    tiles_per_seq = seq // tm
    kern = functools.partial(_qkv_kernel, head0=head0, n_q_heads=n_q_heads, n_kv_heads=n_kv_heads, rope=rope)
    return pl.pallas_call(
        kern,
        grid=(n_tok // tm,),
        in_specs=[
            pl.BlockSpec((tm, d), lambda i: (i, 0)),
            pl.BlockSpec((None, 1, d), lambda i: (layer, 0, 0)),
            _mod_spec(layer, mod_base + 0, d, bidx),
            _mod_spec(layer, mod_base + 1, d, bidx),
            _resident((None, d, width), lambda i: (sub, 0, head0 * HEAD_DIM // width)),
            pl.BlockSpec((None, 1, HEAD_DIM), lambda i: (sub, 0, 0)),
            pl.BlockSpec((None, 1, HEAD_DIM), lambda i: (sub, 0, 0)),
            pl.BlockSpec((tm, HEAD_DIM), lambda i: (i % tiles_per_seq, 0)),
            pl.BlockSpec((tm, HEAD_DIM), lambda i: (i % tiles_per_seq, 0)),
        ],
        out_specs=pl.BlockSpec((None, heads, tm, HEAD_DIM), lambda i: (i // tiles_per_seq, 0, i % tiles_per_seq, 0)),
        out_shape=jax.ShapeDtypeStruct((n_b, heads, seq, HEAD_DIM), BF16),
        compiler_params=_cparams(("arbitrary",)),
        name="qkv",
    )(h, norm_g.reshape(norm_g.shape[0], 1, d), mods, mods, w_qkv,
      q_gain.reshape(q_gain.shape[0], 1, HEAD_DIM), k_gain.reshape(k_gain.shape[0], 1, HEAD_DIM), cos, sin)


def _attn_kernel(q_ref, kl_ref, vl_ref, kc_ref, vc_ref, o_ref):
    rep, tq, hd = q_ref.shape
    dn = (((1,), (1,)), ((), ()))
    c = (HEAD_DIM ** -0.5) * math.log2(math.e)
    blk = min(ATTN_ROWS, tq)
    chains = [(r, slice(sb * blk, (sb + 1) * blk)) for r in range(rep) for sb in range(tq // blk)]

    def scores(chain):
        r, rows = chain
        q = q_ref[r, rows, :]
        return (lax.dot_general(q, kl_ref[...], dn, preferred_element_type=F32),
                lax.dot_general(q, kc_ref[...], dn, preferred_element_type=F32))

    ahead = [scores(ch) for ch in chains[:ATTN_AHEAD]]
    for i, (r, rows) in enumerate(chains):
        s_l, s_c = ahead.pop(0)
        if i + ATTN_AHEAD < len(chains):
            ahead.append(scores(chains[i + ATTN_AHEAD]))
        m = jnp.maximum(jnp.max(s_l, axis=-1, keepdims=True), jnp.max(s_c, axis=-1, keepdims=True))
        p_l = jnp.exp2((s_l - m) * c)
        p_c = jnp.exp2((s_c - m) * c)
        den = jnp.sum(p_l, axis=-1, keepdims=True) + jnp.sum(p_c, axis=-1, keepdims=True)
        o = (jnp.dot(p_l.astype(BF16), vl_ref[...], preferred_element_type=F32)
             + jnp.dot(p_c.astype(BF16), vc_ref[...], preferred_element_type=F32)) / den
        o_ref[rows, r * hd:(r + 1) * hd] = o.astype(BF16)


def _attention(qkv_lat, kv_ctx, n_q_heads, n_kv_heads, *, tq=256):
    n_b, _, seq, hd = qkv_lat.shape
    l_len = kv_ctx.shape[2]
    rep = n_q_heads // n_kv_heads
    return pl.pallas_call(
        _attn_kernel,
        grid=(n_b, n_kv_heads, seq // tq),
        in_specs=[
            pl.BlockSpec((None, rep, tq, hd), lambda b, g, i: (b, g, i, 0)),
            pl.BlockSpec((None, None, seq, hd), lambda b, g, i: (b, n_q_heads + g, 0, 0)),
            pl.BlockSpec((None, None, seq, hd), lambda b, g, i: (b, n_q_heads + n_kv_heads + g, 0, 0)),
            pl.BlockSpec((None, None, l_len, hd), lambda b, g, i: (b, g, 0, 0)),
            pl.BlockSpec((None, None, l_len, hd), lambda b, g, i: (b, n_kv_heads + g, 0, 0)),
        ],
        out_specs=pl.BlockSpec((None, tq, rep * hd), lambda b, g, i: (b, i, g)),
        out_shape=jax.ShapeDtypeStruct((n_b, seq, n_q_heads * hd), BF16),
        compiler_params=_cparams(("arbitrary", "arbitrary", "arbitrary")),
        name="attention",
    )(qkv_lat, qkv_lat, qkv_lat, kv_ctx, kv_ctx)


def _oproj_kernel(x_ref, w_ref, h_ref, gt_ref, o_ref):
    x = x_ref[...]
    for nb in range(o_ref.shape[1] // COL_BLK):
        cols = slice(nb * COL_BLK, (nb + 1) * COL_BLK)
        acc = jnp.dot(x, w_ref[:, cols], preferred_element_type=F32)
        o_ref[:, cols] = h_ref[:, cols] + gt_ref[:, cols] * acc


def _oproj(x, w_o, sub, h, mods, layer, gate_col, *, tm, bidx):
    n_tok, d = h.shape
    k_dim = x.shape[1]
    return pl.pallas_call(
        _oproj_kernel,
        grid=(n_tok // tm,),
        in_specs=[
            pl.BlockSpec((tm, k_dim), lambda i: (i, 0)),
            _resident((None, k_dim, d), lambda i: (sub, 0, 0)),
            pl.BlockSpec((tm, d), lambda i: (i, 0)),
            _mod_spec(layer, gate_col, d, bidx),
        ],
        out_specs=pl.BlockSpec((tm, d), lambda i: (i, 0)),
        out_shape=jax.ShapeDtypeStruct((n_tok, d), F32),
        compiler_params=_cparams(("arbitrary",)),
        name="oproj",
    )(x, w_o, h, mods)


def kernel(x, c, ctx, c_ctx, ada_w, ada_b, norm_ffn1, norm_mix, norm_ffn2, ffn1_w_gate, ffn1_w_up, ffn1_w_down, ffn2_w_gate, ffn2_w_up, ffn2_w_down, s5_a_re, s5_a_im, s5_log_dt, s5_b_re, s5_b_im, s5_c_re, s5_c_im, s5_d, s5_glu_w1, s5_glu_w2, attn_w_qkv, attn_w_o, attn_q_gain, attn_k_gain):
    n_b, seq, d = x.shape
    l_len = ctx.shape[1]
    depth = ada_w.shape[0]
    assert depth == 2 and ada_w.shape[2] == N_MOD * d
    assert n_b == 4, "the S5 scan packs 4 sequences x 2 lane tiles onto the 8 sublanes"
    n_q_heads = d // HEAD_DIM
    n_kv_heads = n_q_heads // KV_REP

    tm_lat = min(1024, seq)
    tm_ctx = min(1024, n_b * l_len)
    tm_mm = min(512, seq)
    tm_qkv = min(256, seq)
    tm_mm_ctx = min(512, l_len)
    ctx_row = n_b
    bidx_lat = lambda tm: (lambda i: (i * tm) // seq)
    bidx_ctx = lambda i: ctx_row

    cond = jnp.concatenate([c, c_ctx[None, :], jnp.zeros((COND_ROWS - n_b - 1, d), F32)], axis=0)
    mods = _ada_mods(cond, ada_w, ada_b).reshape(depth, COND_ROWS, 1, N_MOD * d)

    ffn1 = (ffn1_w_gate, ffn1_w_up, ffn1_w_down)
    ffn2 = (ffn2_w_gate, ffn2_w_up, ffn2_w_down)

    h_lat = x.reshape(n_b * seq, d)
    h_ctx = ctx.reshape(n_b * l_len, d)

    def ffn_pair(h_lat, h_ctx, layer, mod_base, norm_g, weights, cast_next=None, cast_mats=()):
        h_ctx, w_bf16 = _ffn(h_ctx, mods, layer, mod_base, norm_g, *weights, tm=tm_ctx, bidx=bidx_ctx, emit_w=True)
        h_lat, cast = _ffn(h_lat, mods, layer, mod_base, norm_g, *w_bf16, tm=tm_lat, bidx=bidx_lat(tm_lat),
                           cast_next=cast_next, cast_mats=cast_mats)
        return h_lat, h_ctx, cast

    layer = 0
    h_lat, h_ctx, (glu_w1, glu_w2) = ffn_pair(h_lat, h_ctx, layer, 0, norm_ffn1, ffn1,
                                              cast_mats=((s5_glu_w1, 0), (s5_glu_w2, 0)))
    glu_w1, glu_w2 = glu_w1[None], glu_w2[None]

    params = _s5_params(s5_a_re[0], s5_a_im[0], s5_log_dt[0], s5_b_re[0], s5_b_im[0], s5_c_re[0], s5_c_im[0])
    maps = tuple(params[:4]) + (_pair_rows(params[4]), _pair_rows(params[5]))
    h3_lat, h3_ctx = h_lat.reshape(n_b, seq, d), h_ctx.reshape(n_b, l_len, d)
    y_fwd = _s5_scan(h3_lat, h3_ctx, mods, layer, 3, norm_mix, maps, 0)
    z_lat, z_ctx = _s5_scan(h3_lat, h3_ctx, mods, layer, 3, norm_mix, maps, 1, y_prev=y_fwd, d_skip=s5_d)
    h_lat = _glu(z_lat.reshape(n_b * seq, d), glu_w1, glu_w2, 0, h_lat, mods, layer, 5,
                 tm=tm_mm, bidx=bidx_lat(tm_mm))
    h_ctx = _glu(z_ctx.reshape(n_b * l_len, d), glu_w1, glu_w2, 0, h_ctx, mods, layer, 5,
                 tm=tm_mm_ctx, bidx=bidx_ctx)

    h_lat, h_ctx, (w_qkv, w_o) = ffn_pair(h_lat, h_ctx, layer, 6, norm_ffn2, ffn2,
                                          cast_mats=((attn_w_qkv, 0), (attn_w_o, 0)))
    w_qkv, w_o = w_qkv[None], w_o[None]

    layer = 1
    h_lat, h_ctx, ffn2_last = ffn_pair(h_lat, h_ctx, layer, 0, norm_ffn1, ffn1, cast_next=(ffn2, layer))

    cos, sin = _rope_tables(seq)
    qkv_lat = _qkv(h_lat, mods, layer, 3, norm_mix, w_qkv, 0, attn_q_gain, attn_k_gain, cos, sin,
                   seq=seq, tm=tm_qkv, bidx=bidx_lat(tm_qkv), n_q_heads=n_q_heads, n_kv_heads=n_kv_heads,
                   kv_only=False, rope=True)
    no_rope = jnp.zeros((l_len, HEAD_DIM), F32)
    kv_ctx = _qkv(h_ctx, mods, layer, 3, norm_mix, w_qkv, 0, attn_q_gain, attn_k_gain, no_rope, no_rope,
                  seq=l_len, tm=tm_mm_ctx, bidx=bidx_ctx, n_q_heads=n_q_heads, n_kv_heads=n_kv_heads,
                  kv_only=True, rope=False)
    o_lat = _attention(qkv_lat, kv_ctx, n_q_heads, n_kv_heads, tq=min(1024, seq))
    h_lat = _oproj(o_lat.reshape(n_b * seq, d), w_o, 0, h_lat, mods, layer, 5,
                   tm=tm_mm, bidx=bidx_lat(tm_mm))

    h_lat = _ffn(h_lat, mods, layer, 6, norm_ffn2, *ffn2_last, tm=tm_lat, bidx=bidx_lat(tm_lat))
    return h_lat.reshape(n_b, seq, d)
```

```python
import functools
import math

import jax
import jax.numpy as jnp
from jax import lax
from jax.experimental import pallas as pl
from jax.experimental.pallas import tpu as pltpu

F32 = jnp.float32
BF16 = jnp.bfloat16

EPS = 1e-6
N_MOD = 9
GRID_W = 64
HEAD_DIM = 128
KV_REP = 4
ROPE_THETA = 10000.0
S5_GROUP = 16
S5_STATE = 64

V7X_LANES = 128
V7X_SUBLANES = 8
V7X_VMEM_LIMIT = 60 * 1024 * 1024

S5_GB = 8
S5_TQ = 64
S5_PITCH = 68
COND_ROWS = 16
SIDE_COLS = 8
COL_BLK = 512
NORM_ROWS = 32
ATTN_AHEAD = 1
FFN_OUT_BLK = 512
ATTN_ROWS = 512


def _cparams(sem):
    return pltpu.CompilerParams(dimension_semantics=sem, vmem_limit_bytes=V7X_VMEM_LIMIT)


def _resident(shape, index):
    return pl.BlockSpec(shape, index, pipeline_mode=pl.Buffered(1))


def _ada_kernel(a_ref, w_ref, b_ref, o_ref):
    a = a_ref[...]
    s = (a * jax.nn.sigmoid(a)).astype(BF16)
    o_ref[...] = jnp.dot(s, w_ref[...].astype(BF16), preferred_element_type=F32) + b_ref[...]


def _ada_mods(cond, ada_w, ada_b, tn=1024):
    depth, d, n = ada_w.shape
    rows = cond.shape[0]
    return pl.pallas_call(
        _ada_kernel,
        grid=(depth, n // tn),
        in_specs=[
            pl.BlockSpec((rows, d), lambda l, j: (0, 0)),
            pl.BlockSpec((None, d, tn), lambda l, j: (l, 0, j)),
            pl.BlockSpec((None, 1, tn), lambda l, j: (l, 0, j)),
        ],
        out_specs=pl.BlockSpec((None, rows, tn), lambda l, j: (l, 0, j)),
        out_shape=jax.ShapeDtypeStruct((depth, rows, n), F32),
        compiler_params=_cparams(("arbitrary", "arbitrary")),
        name="ada_mods",
    )(cond, ada_w, ada_b.reshape(depth, 1, n))


def _mod_spec(layer, col, d, bidx):
    return pl.BlockSpec((None, None, 1, d), lambda i, *_: (layer, bidx(i), 0, col))


def _norm_mod(x, g, shift, scale):
    r = lax.rsqrt(jnp.mean(x * x, axis=-1, keepdims=True) + EPS)
    return (x * r * g) * (1.0 + scale) + shift


def _norm_mod_blocks(h_ref, nrm_ref, sh_ref, sc_ref, emit):
    g, shift, scale = nrm_ref[...], sh_ref[...], sc_ref[...]

    def body(i, carry):
        rows = pl.ds(pl.multiple_of(i * NORM_ROWS, NORM_ROWS), NORM_ROWS)
        x = h_ref[rows, :]
        emit(rows, x, _norm_mod(x, g, shift, scale))
        return carry

    lax.fori_loop(0, h_ref.shape[0] // NORM_ROWS, body, 0, unroll=4)


def _ffn_kernel(*refs, emit_w, n_side):
    h_ref, nrm_ref, sh_ref, sc_ref, gt_ref, wg_ref, wu_ref, wd_ref = refs[:8]
    side_in = refs[8:8 + n_side]
    o_ref = refs[8 + n_side]
    w_out = refs[9 + n_side:12 + n_side] if emit_w else ()
    side_out = refs[-1 - n_side:-1]
    m_ref = refs[-1]
    f = pl.program_id(1)

    @pl.when(f == 0)
    def _():
        def emit(rows, x, m):
            m_ref[rows, :] = m.astype(BF16)
            o_ref[rows, :] = x

        _norm_mod_blocks(h_ref, nrm_ref, sh_ref, sc_ref, emit)

    for src, dst in zip(side_in, side_out):
        dst[...] = src[...].astype(BF16)

    m = m_ref[...]
    wg, wu = wg_ref[...].astype(BF16), wu_ref[...].astype(BF16)
    if emit_w:
        w_out[0][...] = wg
        w_out[1][...] = wu
    g = jnp.dot(m, wg, preferred_element_type=F32)
    u = jnp.dot(m, wu, preferred_element_type=F32)
    a = (g * jax.nn.sigmoid(g) * u).astype(BF16)
    for nb in range(o_ref.shape[1] // FFN_OUT_BLK):
        cols = slice(nb * FFN_OUT_BLK, (nb + 1) * FFN_OUT_BLK)
        wd = wd_ref[:, cols].astype(BF16)
        if emit_w:
            w_out[2][:, cols] = wd
        p = jnp.dot(a, wd, preferred_element_type=F32)
        o_ref[:, cols] += (0.5 * gt_ref[:, cols]) * p


def _ffn(h, mods, layer, mod_base, norm_g, w_gate, w_up, w_down, *, tm, bidx, emit_w=False, cast_next=None,
         cast_mats=()):
    n_tok, d = h.shape
    f_dim = w_gate.shape[-1]
    if w_gate.ndim == 3:
        tf = 256
        wspec = lambda shape, idx: pl.BlockSpec((None,) + shape, lambda i, f: (layer,) + idx(f))
    else:
        assert not emit_w and w_gate.dtype == BF16
        tf = 512
        wspec = lambda shape, idx: pl.BlockSpec(shape, lambda i, f: idx(f))
    n_i, n_f = n_tok // tm, f_dim // tf
    out_specs = [pl.BlockSpec((tm, d), lambda i, f: (i, 0))]
    out_shape = [jax.ShapeDtypeStruct((n_tok, d), F32)]
    w_shapes = [jax.ShapeDtypeStruct((d, f_dim), BF16), jax.ShapeDtypeStruct((d, f_dim), BF16),
                jax.ShapeDtypeStruct((f_dim, d), BF16)]
    if emit_w:
        assert n_i == 1, "each weight tile must be visited exactly once"
        out_specs += [pl.BlockSpec((d, tf), lambda i, f: (0, f)), pl.BlockSpec((d, tf), lambda i, f: (0, f)),
                      pl.BlockSpec((tf, d), lambda i, f: (f, 0))]
        out_shape += w_shapes
    side_args, side_specs = [], []
    if cast_next is not None:
        assert not emit_w
        side_args, s_layer = list(cast_next[0]), cast_next[1]
        rb, cb = d // n_i, f_dim // n_f
        side_specs = [pl.BlockSpec((None, rb, cb), lambda i, f: (s_layer, i, f)),
                      pl.BlockSpec((None, rb, cb), lambda i, f: (s_layer, i, f)),
                      pl.BlockSpec((None, cb, rb), lambda i, f: (s_layer, f, i))]
        out_specs += [pl.BlockSpec((rb, cb), lambda i, f: (i, f)), pl.BlockSpec((rb, cb), lambda i, f: (i, f)),
                      pl.BlockSpec((cb, rb), lambda i, f: (f, i))]
        out_shape += w_shapes
    for arr, sub in cast_mats:
        assert not emit_w and n_f >= SIDE_COLS
        rb, cb = arr.shape[1] // n_i, arr.shape[2] // SIDE_COLS
        col = lambda f: jnp.minimum(f, SIDE_COLS - 1)
        side_args.append(arr)
        side_specs.append(pl.BlockSpec((None, rb, cb), lambda i, f, sub=sub: (sub, i, col(f))))
        out_specs.append(pl.BlockSpec((rb, cb), lambda i, f: (i, col(f))))
        out_shape.append(jax.ShapeDtypeStruct(arr.shape[1:], BF16))
    out = pl.pallas_call(
        functools.partial(_ffn_kernel, emit_w=emit_w, n_side=len(side_args)),
        grid=(n_i, n_f),
        in_specs=[
            pl.BlockSpec((tm, d), lambda i, f: (i, 0), pipeline_mode=pl.Buffered(1 if emit_w else 2)),
            pl.BlockSpec((None, 1, d), lambda i, f: (layer, 0, 0)),
            _mod_spec(layer, mod_base + 0, d, bidx),
            _mod_spec(layer, mod_base + 1, d, bidx),
            _mod_spec(layer, mod_base + 2, d, bidx),
            wspec((d, tf), lambda f: (0, f)),
            wspec((d, tf), lambda f: (0, f)),
            wspec((tf, d), lambda f: (f, 0)),
        ] + side_specs,
        out_specs=out_specs,
        out_shape=out_shape,
        scratch_shapes=[pltpu.VMEM((tm, d), BF16)],
        compiler_params=_cparams(("arbitrary", "arbitrary")),
        name="ffn",
    )(h, norm_g.reshape(norm_g.shape[0], 1, d), mods, mods, mods, w_gate, w_up, w_down, *side_args)
    return (out[0], tuple(out[1:])) if len(out) > 1 else out[0]


def _zoh(a_re, a_im, log_dt):
    dt = jnp.exp(log_dt)
    mag = jnp.exp(a_re * dt)
    l_re = mag * jnp.cos(a_im * dt)
    l_im = mag * jnp.sin(a_im * dt)
    den = a_re * a_re + a_im * a_im
    c_re = ((l_re - 1.0) * a_re + l_im * a_im) / den
    c_im = (l_im * a_re - (l_re - 1.0) * a_im) / den
    return l_re, l_im, c_re, c_im


def _lane_tile(x, reps):
    w = x.shape[1]
    row = lax.broadcasted_iota(jnp.int32, (w, w * reps), 0)
    col = lax.broadcasted_iota(jnp.int32, (w, w * reps), 1)
    sel = (row == col % w).astype(BF16)
    return jnp.dot(x, sel, preferred_element_type=F32)


def _s5_param_kernel(are_ref, aim_ref, ldt_ref, bre_ref, bim_ref, cre_ref, cim_ref, ar2_ref, ai2_ref, ld2_ref,
                     obr_ref, obi_ref, ocr_ref, oci_ref, olr_ref, oli_ref):
    gpb = are_ref.shape[0] // S5_GROUP
    _, _, k_re, k_im = _zoh(are_ref[...], aim_ref[...], ldt_ref[...])
    bb_re = k_re * bre_ref[...] - k_im * bim_ref[...]
    bb_im = k_re * bim_ref[...] + k_im * bre_ref[...]
    n_in, n_st = bb_re.shape[0], S5_STATE * gpb
    on_diag = (lax.broadcasted_iota(jnp.int32, (n_in, n_st), 0) // S5_GROUP
               == lax.broadcasted_iota(jnp.int32, (n_in, n_st), 1) // S5_STATE)
    obr_ref[...] = jnp.where(on_diag, _lane_tile(bb_re.astype(BF16), gpb), 0.0).astype(BF16)
    obi_ref[...] = jnp.where(on_diag, _lane_tile(bb_im.astype(BF16), gpb), 0.0).astype(BF16)
    on_diag_t = (lax.broadcasted_iota(jnp.int32, (n_st, n_in), 0) // S5_STATE
                 == lax.broadcasted_iota(jnp.int32, (n_st, n_in), 1) // S5_GROUP)
    ocr_ref[...] = jnp.where(on_diag_t, _lane_tile(cre_ref[...].astype(BF16), gpb), 0.0).astype(BF16)
    oci_ref[...] = jnp.where(on_diag_t, _lane_tile((-cim_ref[...]).astype(BF16), gpb), 0.0).astype(BF16)
    l_re, l_im, _, _ = _zoh(ar2_ref[...], ai2_ref[...], ld2_ref[...])
    olr_ref[...] = l_re
    oli_ref[...] = l_im


def _s5_params(a_re, a_im, log_dt, b_re, b_im, c_re, c_im):
    dirs, g, p = a_re.shape
    hch = b_re.shape[3]
    gpb = g // S5_GB
    rows_in, rows_st = gpb * hch, gpb * p

    def per_channel(v):
        return jnp.broadcast_to(v[:, :, None, :], (dirs, g, hch, p)).reshape(dirs, S5_GB, rows_in, p)

    ldt = jnp.broadcast_to(log_dt[:, :, None], (dirs, g, p))
    args = (
        per_channel(a_re), per_channel(a_im), per_channel(ldt),
        jnp.swapaxes(b_re, 2, 3).reshape(dirs, S5_GB, rows_in, p),
        jnp.swapaxes(b_im, 2, 3).reshape(dirs, S5_GB, rows_in, p),
        jnp.swapaxes(c_re, 2, 3).reshape(dirs, S5_GB, rows_st, hch),
        jnp.swapaxes(c_im, 2, 3).reshape(dirs, S5_GB, rows_st, hch),
        a_re.reshape(dirs, g * p // V7X_LANES, V7X_LANES),
        a_im.reshape(dirs, g * p // V7X_LANES, V7X_LANES),
        ldt.reshape(dirs, g * p // V7X_LANES, V7X_LANES),
    )
    blk = lambda r, c: pl.BlockSpec((None, None, r, c), lambda dd, gb: (dd, gb, 0, 0))
    flat = pl.BlockSpec((None, g * p // V7X_LANES, V7X_LANES), lambda dd, gb: (dd, 0, 0))
    return pl.pallas_call(
        _s5_param_kernel,
        grid=(dirs, S5_GB),
        in_specs=[blk(rows_in, p)] * 5 + [blk(rows_st, hch)] * 2 + [flat] * 3,
        out_specs=[blk(rows_in, rows_st), blk(rows_in, rows_st), blk(rows_st, rows_in), blk(rows_st, rows_in),
                   flat, flat],
        out_shape=[
            jax.ShapeDtypeStruct((dirs, S5_GB, rows_in, rows_st), BF16),
            jax.ShapeDtypeStruct((dirs, S5_GB, rows_in, rows_st), BF16),
            jax.ShapeDtypeStruct((dirs, S5_GB, rows_st, rows_in), BF16),
            jax.ShapeDtypeStruct((dirs, S5_GB, rows_st, rows_in), BF16),
            jax.ShapeDtypeStruct((dirs, g * p // V7X_LANES, V7X_LANES), F32),
            jax.ShapeDtypeStruct((dirs, g * p // V7X_LANES, V7X_LANES), F32),
        ],
        compiler_params=_cparams(("arbitrary", "arbitrary")),
        name="s5_params",
    )(*args)


def _pair_rows(lam):
    dirs, tiles, lanes = lam.shape
    half = V7X_SUBLANES // 2
    x = jnp.broadcast_to(lam.reshape(dirs, 2, tiles // 2, 1, lanes), (dirs, 2, tiles // 2, half, lanes))
    return jnp.transpose(x, (0, 2, 1, 3, 4)).reshape(dirs, tiles // 2, V7X_SUBLANES, lanes)


def _s5_scan_kernel(*refs, n_ctx_chunks, tq, pitch, reverse, ctx_row, emit_z):
    gb_per_half = S5_GB // 2
    n_in = 16 if emit_z else 13
    (hl_ref, hc_ref, nrm_ref, sh_ref, sc_ref, bre_ref, bim_ref, cre_ref, cim_ref, are_ref, aim_ref) = refs[:11]
    if emit_z:
        yl_ref, yc_ref, dsk_ref = refs[11:14]
    ol_ref, oc_ref = refs[n_in - 2:n_in]
    u_st = refs[n_in]
    s_re_k = refs[n_in + 1:n_in + 1 + gb_per_half]
    s_im_k = refs[n_in + 1 + gb_per_half:n_in + 1 + 2 * gb_per_half]
    h_re, h_im = refs[n_in + 1 + 2 * gb_per_half:]
    c = pl.program_id(0)
    n_b = hl_ref.shape[0]
    half_rows = n_b * pitch
    tiles_per_gb = s_re_k[0].shape[0]
    lanes_in = u_st.shape[1] // S5_GB

    @pl.when(c == 0)
    def _():
        h_re[...] = jnp.zeros_like(h_re)
        h_im[...] = jnp.zeros_like(h_im)
        u_st[...] = jnp.zeros_like(u_st)

    is_ctx = c < n_ctx_chunks

    @pl.when(is_ctx)
    def _():
        for b in range(n_b):
            u_st[b * pitch:b * pitch + tq, :] = _norm_mod(hc_ref[b], nrm_ref[...], sh_ref[ctx_row], sc_ref[ctx_row])

    @pl.when(jnp.logical_not(is_ctx))
    def _():
        for b in range(n_b):
            u_st[b * pitch:b * pitch + tq, :] = _norm_mod(hl_ref[b], nrm_ref[...], sh_ref[b], sc_ref[b])

    def project_in(gb):
        lhs = u_st[:, gb * lanes_in:(gb + 1) * lanes_in].astype(BF16)
        p_re = jnp.dot(lhs, bre_ref[gb], preferred_element_type=F32)
        p_im = jnp.dot(lhs, bim_ref[gb], preferred_element_type=F32)
        half = gb // gb_per_half
        rows = slice(half * half_rows, (half + 1) * half_rows)
        s_re, s_im = s_re_k[gb % gb_per_half], s_im_k[gb % gb_per_half]
        for j in range(tiles_per_gb):
            s_re[j, rows, :] = p_re[:, j * V7X_LANES:(j + 1) * V7X_LANES]
            s_im[j, rows, :] = p_im[:, j * V7X_LANES:(j + 1) * V7X_LANES]

    def scan_pass(k):
        s_re, s_im = s_re_k[k], s_im_k[k]
        ns = list(range(k * tiles_per_gb, (k + 1) * tiles_per_gb))
        a_r = [are_ref[n] for n in ns]
        a_i = [aim_ref[n] for n in ns]
        hr = [h_re[n] for n in ns]
        hi = [h_im[n] for n in ns]
        for t in range(tq):
            rows = pl.ds(tq - 1 - t if reverse else t, V7X_SUBLANES, stride=pitch)
            for j in range(tiles_per_gb):
                n_r = a_r[j] * hr[j] - a_i[j] * hi[j] + s_re[j, rows, :]
                n_i = a_r[j] * hi[j] + a_i[j] * hr[j] + s_im[j, rows, :]
                s_re[j, rows, :] = n_r
                s_im[j, rows, :] = n_i
                hr[j], hi[j] = n_r, n_i
        for j, n in enumerate(ns):
            h_re[n] = hr[j]
            h_im[n] = hi[j]

    def project_out(gb):
        half = gb // gb_per_half
        rows = slice(half * half_rows, (half + 1) * half_rows)
        s_re, s_im = s_re_k[gb % gb_per_half], s_im_k[gb % gb_per_half]
        l_re = jnp.concatenate([s_re[j, rows, :] for j in range(tiles_per_gb)], axis=1).astype(BF16)
        l_im = jnp.concatenate([s_im[j, rows, :] for j in range(tiles_per_gb)], axis=1).astype(BF16)
        return (jnp.dot(l_re, cre_ref[gb], preferred_element_type=F32)
                + jnp.dot(l_im, cim_ref[gb], preferred_element_type=F32))

    for gb in range(S5_GB):
        project_in(gb)
    for k in range(gb_per_half):
        scan_pass(k)
    ys = [project_out(gb) for gb in range(S5_GB)]

    def emit(out_ref, yf_ref):
        for gb in range(S5_GB):
            cols = slice(gb * lanes_in, (gb + 1) * lanes_in)
            for b in range(n_b):
                y = ys[gb][b * pitch:b * pitch + tq]
                if emit_z:
                    y = y + yf_ref[b, :, cols] + dsk_ref[:, cols] * u_st[b * pitch:b * pitch + tq, cols]
                    out_ref[b, :, cols] = _gelu_tanh(y).astype(out_ref.dtype)
                else:
                    out_ref[b, :, cols] = y

    @pl.when(is_ctx)
    def _():
        emit(oc_ref, yc_ref if emit_z else None)

    @pl.when(jnp.logical_not(is_ctx))
    def _():
        emit(ol_ref, yl_ref if emit_z else None)


def _gelu_tanh(x):
    return 0.5 * x * (1.0 + jnp.tanh(math.sqrt(2.0 / math.pi) * (x + 0.044715 * (x * x * x))))


def _s5_scan(h_lat, h_ctx, mods, layer, mod_base, norm_g, maps, direction, y_prev=None, d_skip=None):
    n_b, s_len, d = h_lat.shape
    l_len = h_ctx.shape[1]
    tq, pitch = S5_TQ, S5_PITCH
    n_l, n_s = l_len // tq, s_len // tq
    bd_bre, bd_bim, bd_cre, bd_cim, a_re, a_im = maps
    n_pairs = a_re.shape[1]
    reverse = direction == 1
    emit_z = y_prev is not None

    def lat_idx(c):
        k = jnp.maximum(c - n_l, 0)
        return n_s - 1 - k if reverse else k

    def ctx_idx(c):
        k = jnp.minimum(c, n_l - 1)
        return n_l - 1 - k if reverse else k

    def wspec(arr):
        shape = arr.shape[1:]
        return _resident((None,) + shape, lambda c: (direction,) + (0,) * len(shape))

    lat_spec = pl.BlockSpec((n_b, tq, d), lambda c: (0, lat_idx(c), 0))
    ctx_spec = pl.BlockSpec((n_b, tq, d), lambda c: (0, ctx_idx(c), 0))
    mod_rows = lambda col: pl.BlockSpec((None, COND_ROWS, 1, d), lambda c: (layer, 0, 0, col))
    in_specs = [lat_spec, ctx_spec,
                pl.BlockSpec((None, 1, d), lambda c: (layer, 0, 0)),
                mod_rows(mod_base), mod_rows(mod_base + 1),
                wspec(bd_bre), wspec(bd_bim), wspec(bd_cre), wspec(bd_cim), wspec(a_re), wspec(a_im)]
    args = [h_lat, h_ctx, norm_g.reshape(norm_g.shape[0], 1, d), mods, mods,
            bd_bre, bd_bim, bd_cre, bd_cim, a_re, a_im]
    out_dtype = F32
    if emit_z:
        in_specs += [lat_spec, ctx_spec, pl.BlockSpec((None, 1, d), lambda c: (0, 0, 0))]
        args += [y_prev[0], y_prev[1], d_skip.reshape(d_skip.shape[0], 1, d)]
        out_dtype = BF16
    kern = functools.partial(_s5_scan_kernel, n_ctx_chunks=n_l, tq=tq, pitch=pitch, reverse=reverse,
                             ctx_row=n_b, emit_z=emit_z)
    return pl.pallas_call(
        kern,
        grid=(n_l + n_s,),
        in_specs=in_specs,
        out_specs=[lat_spec, ctx_spec],
        out_shape=[jax.ShapeDtypeStruct((n_b, s_len, d), out_dtype),
                   jax.ShapeDtypeStruct((n_b, l_len, d), out_dtype)],
        scratch_shapes=[
            pltpu.VMEM((n_b * pitch, d), F32),
            *[pltpu.VMEM((2 * n_pairs // S5_GB, 2 * n_b * pitch, V7X_LANES), F32) for _ in range(S5_GB)],
            pltpu.VMEM((n_pairs, V7X_SUBLANES, V7X_LANES), F32),
            pltpu.VMEM((n_pairs, V7X_SUBLANES, V7X_LANES), F32),
        ],
        compiler_params=_cparams(("arbitrary",)),
        name="s5_bwd" if reverse else "s5_fwd",
    )(*args)


def _glu_kernel(z_ref, w1_ref, w2_ref, h_ref, gt_ref, o_ref):
    z = z_ref[...]
    for nb in range(o_ref.shape[1] // COL_BLK):
        cols = slice(nb * COL_BLK, (nb + 1) * COL_BLK)
        a = jnp.dot(z, w1_ref[:, cols], preferred_element_type=F32)
        b = jnp.dot(z, w2_ref[:, cols], preferred_element_type=F32)
        o_ref[:, cols] = h_ref[:, cols] + gt_ref[:, cols] * (a * jax.nn.sigmoid(b))


def _glu(z, w1, w2, sub, h, mods, layer, gate_col, *, tm, bidx):
    n_tok, d = h.shape
    return pl.pallas_call(
        _glu_kernel,
        grid=(n_tok // tm,),
        in_specs=[
            pl.BlockSpec((tm, d), lambda i: (i, 0)),
            _resident((None, d, d), lambda i: (sub, 0, 0)),
            _resident((None, d, d), lambda i: (sub, 0, 0)),
            pl.BlockSpec((tm, d), lambda i: (i, 0)),
            _mod_spec(layer, gate_col, d, bidx),
        ],
        out_specs=pl.BlockSpec((tm, d), lambda i: (i, 0)),
        out_shape=jax.ShapeDtypeStruct((n_tok, d), F32),
        compiler_params=_cparams(("arbitrary",)),
        name="s5_glu",
    )(z, w1, w2, h, mods)


def _rope_tables(seq):
    pairs = HEAD_DIM // 4
    freqs = ROPE_THETA ** (-jnp.arange(pairs, dtype=F32) / pairs)
    pos = jnp.arange(seq, dtype=jnp.int32)
    ang_r = (pos // GRID_W).astype(F32)[:, None] * freqs
    ang_c = (pos % GRID_W).astype(F32)[:, None] * freqs
    cos = jnp.concatenate([jnp.cos(ang_r), jnp.cos(ang_r), jnp.cos(ang_c), jnp.cos(ang_c)], axis=-1)
    sin = jnp.concatenate([-jnp.sin(ang_r), jnp.sin(ang_r), -jnp.sin(ang_c), jnp.sin(ang_c)], axis=-1)
    return cos, sin


def _qkv_kernel(h_ref, nrm_ref, sh_ref, sc_ref, w_ref, qg_ref, kg_ref, cos_ref, sin_ref, o_ref,
                *, head0, n_q_heads, n_kv_heads, rope):
    m = _norm_mod(h_ref[...], nrm_ref[...], sh_ref[...], sc_ref[...]).astype(BF16)
    lane = lax.broadcasted_iota(jnp.int32, (1, HEAD_DIM), 1)
    first = (lane % (HEAD_DIM // 2)) < (HEAD_DIM // 4)
    hpb = COL_BLK // HEAD_DIM
    for nb in range(w_ref.shape[1] // COL_BLK):
        acc = jnp.dot(m, w_ref[:, nb * COL_BLK:(nb + 1) * COL_BLK], preferred_element_type=F32)
        for hh in range(hpb):
            head = head0 + nb * hpb + hh
            x = acc[:, hh * HEAD_DIM:(hh + 1) * HEAD_DIM]
            if head < n_q_heads + n_kv_heads:
                gain = qg_ref[...] if head < n_q_heads else kg_ref[...]
                xn = x * lax.rsqrt(jnp.mean(x * x, axis=-1, keepdims=True) + EPS) * gain
                if rope:
                    partner = jnp.where(first, pltpu.roll(xn, HEAD_DIM - HEAD_DIM // 4, 1),
                                        pltpu.roll(xn, HEAD_DIM // 4, 1))
                    xn = xn * cos_ref[...] + partner * sin_ref[...]
                x = xn
            o_ref[nb * hpb + hh] = x.astype(BF16)


def _qkv(h, mods, layer, mod_base, norm_g, w_qkv, sub, q_gain, k_gain, cos, sin, *, seq, tm, bidx,
         n_q_heads, n_kv_heads, kv_only, rope):
    n_tok, d = h.shape
    n_b = n_tok // seq
    n_cols = w_qkv.shape[2]
    head0 = n_q_heads if kv_only else 0
    width = n_cols - head0 * HEAD_DIM
    heads = width // HEAD_DIM
    assert (head0 * HEAD_DIM) % width == 0
    tiles_per_seq = seq // tm
    kern = functools.partial(_qkv_kernel, head0=head0, n_q_heads=n_q_heads, n_kv_heads=n_kv_heads, rope=rope)
    return pl.pallas_call(
        kern,
        grid=(n_tok // tm,),
        in_specs=[
            pl.BlockSpec((tm, d), lambda i: (i, 0)),
            pl.BlockSpec((None, 1, d), lambda i: (layer, 0, 0)),
            _mod_spec(layer, mod_base + 0, d, bidx),
            _mod_spec(layer, mod_base + 1, d, bidx),
            _resident((None, d, width), lambda i: (sub, 0, head0 * HEAD_DIM // width)),
            pl.BlockSpec((None, 1, HEAD_DIM), lambda i: (sub, 0, 0)),
            pl.BlockSpec((None, 1, HEAD_DIM), lambda i: (sub, 0, 0)),
            pl.BlockSpec((tm, HEAD_DIM), lambda i: (i % tiles_per_seq, 0)),
            pl.BlockSpec((tm, HEAD_DIM), lambda i: (i % tiles_per_seq, 0)),
        ],
        out_specs=pl.BlockSpec((None, heads, tm, HEAD_DIM), lambda i: (i // tiles_per_seq, 0, i % tiles_per_seq, 0)),
        out_shape=jax.ShapeDtypeStruct((n_b, heads, seq, HEAD_DIM), BF16),
        compiler_params=_cparams(("arbitrary",)),
        name="qkv",
    )(h, norm_g.reshape(norm_g.shape[0], 1, d), mods, mods, w_qkv,
      q_gain.reshape(q_gain.shape[0], 1, HEAD_DIM), k_gain.reshape(k_gain.shape[0], 1, HEAD_DIM), cos, sin)


def _attn_kernel(q_ref, kl_ref, vl_ref, kc_ref, vc_ref, o_ref):
    rep, tq, hd = q_ref.shape
    dn = (((1,), (1,)), ((), ()))
    c = (HEAD_DIM ** -0.5) * math.log2(math.e)
    blk = min(ATTN_ROWS, tq)
    chains = [(r, slice(sb * blk, (sb + 1) * blk)) for r in range(rep) for sb in range(tq // blk)]

    def scores(chain):
        r, rows = chain
        q = q_ref[r, rows, :]
        return (lax.dot_general(q, kl_ref[...], dn, preferred_element_type=F32),
                lax.dot_general(q, kc_ref[...], dn, preferred_element_type=F32))

    ahead = [scores(ch) for ch in chains[:ATTN_AHEAD]]
    for i, (r, rows) in enumerate(chains):
        s_l, s_c = ahead.pop(0)
        if i + ATTN_AHEAD < len(chains):
            ahead.append(scores(chains[i + ATTN_AHEAD]))
        m = jnp.maximum(jnp.max(s_l, axis=-1, keepdims=True), jnp.max(s_c, axis=-1, keepdims=True))
        p_l = jnp.exp2((s_l - m) * c)
        p_c = jnp.exp2((s_c - m) * c)
        den = jnp.sum(p_l, axis=-1, keepdims=True) + jnp.sum(p_c, axis=-1, keepdims=True)
        o = (jnp.dot(p_l.astype(BF16), vl_ref[...], preferred_element_type=F32)
             + jnp.dot(p_c.astype(BF16), vc_ref[...], preferred_element_type=F32)) / den
        o_ref[rows, r * hd:(r + 1) * hd] = o.astype(BF16)


def _attention(qkv_lat, kv_ctx, n_q_heads, n_kv_heads, *, tq=256):
    n_b, _, seq, hd = qkv_lat.shape
    l_len = kv_ctx.shape[2]
    rep = n_q_heads // n_kv_heads
    return pl.pallas_call(
        _attn_kernel,
        grid=(n_b, n_kv_heads, seq // tq),
        in_specs=[
            pl.BlockSpec((None, rep, tq, hd), lambda b, g, i: (b, g, i, 0)),
            pl.BlockSpec((None, None, seq, hd), lambda b, g, i: (b, n_q_heads + g, 0, 0)),
            pl.BlockSpec((None, None, seq, hd), lambda b, g, i: (b, n_q_heads + n_kv_heads + g, 0, 0)),
            pl.BlockSpec((None, None, l_len, hd), lambda b, g, i: (b, g, 0, 0)),
            pl.BlockSpec((None, None, l_len, hd), lambda b, g, i: (b, n_kv_heads + g, 0, 0)),
        ],
        out_specs=pl.BlockSpec((None, tq, rep * hd), lambda b, g, i: (b, i, g)),
        out_shape=jax.ShapeDtypeStruct((n_b, seq, n_q_heads * hd), BF16),
        compiler_params=_cparams(("arbitrary", "arbitrary", "arbitrary")),
        name="attention",
    )(qkv_lat, qkv_lat, qkv_lat, kv_ctx, kv_ctx)


def _oproj_kernel(x_ref, w_ref, h_ref, gt_ref, o_ref):
    x = x_ref[...]
    for nb in range(o_ref.shape[1] // COL_BLK):
        cols = slice(nb * COL_BLK, (nb + 1) * COL_BLK)
        acc = jnp.dot(x, w_ref[:, cols], preferred_element_type=F32)
        o_ref[:, cols] = h_ref[:, cols] + gt_ref[:, cols] * acc


def _oproj(x, w_o, sub, h, mods, layer, gate_col, *, tm, bidx):
    n_tok, d = h.shape
    k_dim = x.shape[1]
    return pl.pallas_call(
        _oproj_kernel,
        grid=(n_tok // tm,),
        in_specs=[
            pl.BlockSpec((tm, k_dim), lambda i: (i, 0)),
            _resident((None, k_dim, d), lambda i: (sub, 0, 0)),
            pl.BlockSpec((tm, d), lambda i: (i, 0)),
            _mod_spec(layer, gate_col, d, bidx),
        ],
        out_specs=pl.BlockSpec((tm, d), lambda i: (i, 0)),
        out_shape=jax.ShapeDtypeStruct((n_tok, d), F32),
        compiler_params=_cparams(("arbitrary",)),
        name="oproj",
    )(x, w_o, h, mods)


def kernel(x, c, ctx, c_ctx, ada_w, ada_b, norm_ffn1, norm_mix, norm_ffn2, ffn1_w_gate, ffn1_w_up, ffn1_w_down, ffn2_w_gate, ffn2_w_up, ffn2_w_down, s5_a_re, s5_a_im, s5_log_dt, s5_b_re, s5_b_im, s5_c_re, s5_c_im, s5_d, s5_glu_w1, s5_glu_w2, attn_w_qkv, attn_w_o, attn_q_gain, attn_k_gain):
    n_b, seq, d = x.shape
    l_len = ctx.shape[1]
    depth = ada_w.shape[0]
    assert depth == 2 and ada_w.shape[2] == N_MOD * d
    assert n_b == 4, "the S5 scan packs 4 sequences x 2 lane tiles onto the 8 sublanes"
    n_q_heads = d // HEAD_DIM
    n_kv_heads = n_q_heads // KV_REP

    tm_lat = min(1024, seq)
    tm_ctx = min(1024, n_b * l_len)
    tm_mm = min(512, seq)
    tm_qkv = min(256, seq)
    tm_mm_ctx = min(512, l_len)
    ctx_row = n_b
    bidx_lat = lambda tm: (lambda i: (i * tm) // seq)
    bidx_ctx = lambda i: ctx_row

    cond = jnp.concatenate([c, c_ctx[None, :], jnp.zeros((COND_ROWS - n_b - 1, d), F32)], axis=0)
    mods = _ada_mods(cond, ada_w, ada_b).reshape(depth, COND_ROWS, 1, N_MOD * d)

    ffn1 = (ffn1_w_gate, ffn1_w_up, ffn1_w_down)
    ffn2 = (ffn2_w_gate, ffn2_w_up, ffn2_w_down)

    h_lat = x.reshape(n_b * seq, d)
    h_ctx = ctx.reshape(n_b * l_len, d)

    def ffn_pair(h_lat, h_ctx, layer, mod_base, norm_g, weights, cast_next=None, cast_mats=()):
        h_ctx, w_bf16 = _ffn(h_ctx, mods, layer, mod_base, norm_g, *weights, tm=tm_ctx, bidx=bidx_ctx, emit_w=True)
        h_lat, cast = _ffn(h_lat, mods, layer, mod_base, norm_g, *w_bf16, tm=tm_lat, bidx=bidx_lat(tm_lat),
                           cast_next=cast_next, cast_mats=cast_mats)
        return h_lat, h_ctx, cast

    layer = 0
    h_lat, h_ctx, (glu_w1, glu_w2) = ffn_pair(h_lat, h_ctx, layer, 0, norm_ffn1, ffn1,
                                              cast_mats=((s5_glu_w1, 0), (s5_glu_w2, 0)))
    glu_w1, glu_w2 = glu_w1[None], glu_w2[None]

    params = _s5_params(s5_a_re[0], s5_a_im[0], s5_log_dt[0], s5_b_re[0], s5_b_im[0], s5_c_re[0], s5_c_im[0])
    maps = tuple(params[:4]) + (_pair_rows(params[4]), _pair_rows(params[5]))
    h3_lat, h3_ctx = h_lat.reshape(n_b, seq, d), h_ctx.reshape(n_b, l_len, d)
    y_fwd = _s5_scan(h3_lat, h3_ctx, mods, layer, 3, norm_mix, maps, 0)
    z_lat, z_ctx = _s5_scan(h3_lat, h3_ctx, mods, layer, 3, norm_mix, maps, 1, y_prev=y_fwd, d_skip=s5_d)
    h_lat = _glu(z_lat.reshape(n_b * seq, d), glu_w1, glu_w2, 0, h_lat, mods, layer, 5,
                 tm=tm_mm, bidx=bidx_lat(tm_mm))
    h_ctx = _glu(z_ctx.reshape(n_b * l_len, d), glu_w1, glu_w2, 0, h_ctx, mods, layer, 5,
                 tm=tm_mm_ctx, bidx=bidx_ctx)

    h_lat, h_ctx, (w_qkv, w_o) = ffn_pair(h_lat, h_ctx, layer, 6, norm_ffn2, ffn2,
                                          cast_mats=((attn_w_qkv, 0), (attn_w_o, 0)))
    w_qkv, w_o = w_qkv[None], w_o[None]

    layer = 1
    h_lat, h_ctx, ffn2_last = ffn_pair(h_lat, h_ctx, layer, 0, norm_ffn1, ffn1, cast_next=(ffn2, layer))

    cos, sin = _rope_tables(seq)
    qkv_lat = _qkv(h_lat, mods, layer, 3, norm_mix, w_qkv, 0, attn_q_gain, attn_k_gain, cos, sin,
                   seq=seq, tm=tm_qkv, bidx=bidx_lat(tm_qkv), n_q_heads=n_q_heads, n_kv_heads=n_kv_heads,
                   kv_only=False, rope=True)
    no_rope = jnp.zeros((l_len, HEAD_DIM), F32)
    kv_ctx = _qkv(h_ctx, mods, layer, 3, norm_mix, w_qkv, 0, attn_q_gain, attn_k_gain, no_rope, no_rope,
                  seq=l_len, tm=tm_mm_ctx, bidx=bidx_ctx, n_q_heads=n_q_heads, n_kv_heads=n_kv_heads,
                  kv_only=True, rope=False)
    o_lat = _attention(qkv_lat, kv_ctx, n_q_heads, n_kv_heads, tq=min(1024, seq))
    h_lat = _oproj(o_lat.reshape(n_b * seq, d), w_o, 0, h_lat, mods, layer, 5,
                   tm=tm_mm, bidx=bidx_lat(tm_mm))

    h_lat = _ffn(h_lat, mods, layer, 6, norm_ffn2, *ffn2_last, tm=tm_lat, bidx=bidx_lat(tm_lat))
    return h_lat.reshape(n_b, seq, d)
```

```python
import functools
import math

import jax
import jax.numpy as jnp
from jax import lax
from jax.experimental import pallas as pl
from jax.experimental.pallas import tpu as pltpu

F32 = jnp.float32
BF16 = jnp.bfloat16

EPS = 1e-6
N_MOD = 9
GRID_W = 64
HEAD_DIM = 128
KV_REP = 4
ROPE_THETA = 10000.0
S5_GROUP = 16
S5_STATE = 64

V7X_LANES = 128
V7X_SUBLANES = 8
V7X_VMEM_LIMIT = 60 * 1024 * 1024

S5_GB = 8
S5_TQ = 64
S5_PITCH = 68
COND_ROWS = 16
SIDE_COLS = 8
COL_BLK = 512
NORM_ROWS = 32
ATTN_AHEAD = 1
FFN_OUT_BLK = 512
ATTN_ROWS = 512


def _cparams(sem):
    return pltpu.CompilerParams(dimension_semantics=sem, vmem_limit_bytes=V7X_VMEM_LIMIT)


def _resident(shape, index):
    return pl.BlockSpec(shape, index, pipeline_mode=pl.Buffered(1))


def _ada_kernel(a_ref, w_ref, b_ref, o_ref):
    a = a_ref[...]
    s = (a * jax.nn.sigmoid(a)).astype(BF16)
    o_ref[...] = jnp.dot(s, w_ref[...].astype(BF16), preferred_element_type=F32) + b_ref[...]


def _ada_mods(cond, ada_w, ada_b, tn=1024):
    depth, d, n = ada_w.shape
    rows = cond.shape[0]
    return pl.pallas_call(
        _ada_kernel,
        grid=(depth, n // tn),
        in_specs=[
            pl.BlockSpec((rows, d), lambda l, j: (0, 0)),
            pl.BlockSpec((None, d, tn), lambda l, j: (l, 0, j)),
            pl.BlockSpec((None, 1, tn), lambda l, j: (l, 0, j)),
        ],
        out_specs=pl.BlockSpec((None, rows, tn), lambda l, j: (l, 0, j)),
        out_shape=jax.ShapeDtypeStruct((depth, rows, n), F32),
        compiler_params=_cparams(("arbitrary", "arbitrary")),
        name="ada_mods",
    )(cond, ada_w, ada_b.reshape(depth, 1, n))


def _mod_spec(layer, col, d, bidx):
    return pl.BlockSpec((None, None, 1, d), lambda i, *_: (layer, bidx(i), 0, col))


def _norm_mod(x, g, shift, scale):
    r = lax.rsqrt(jnp.mean(x * x, axis=-1, keepdims=True) + EPS)
    return (x * r * g) * (1.0 + scale) + shift


def _norm_mod_blocks(h_ref, nrm_ref, sh_ref, sc_ref, emit):
    g, shift, scale = nrm_ref[...], sh_ref[...], sc_ref[...]

    def body(i, carry):
        rows = pl.ds(pl.multiple_of(i * NORM_ROWS, NORM_ROWS), NORM_ROWS)
        x = h_ref[rows, :]
        emit(rows, x, _norm_mod(x, g, shift, scale))
        return carry

    lax.fori_loop(0, h_ref.shape[0] // NORM_ROWS, body, 0, unroll=4)


def _ffn_kernel(*refs, emit_w, n_side):
    h_ref, nrm_ref, sh_ref, sc_ref, gt_ref, wg_ref, wu_ref, wd_ref = refs[:8]
    side_in = refs[8:8 + n_side]
    o_ref = refs[8 + n_side]
    w_out = refs[9 + n_side:12 + n_side] if emit_w else ()
    side_out = refs[-1 - n_side:-1]
    m_ref = refs[-1]
    f = pl.program_id(1)

    @pl.when(f == 0)
    def _():
        def emit(rows, x, m):
            m_ref[rows, :] = m.astype(BF16)
            o_ref[rows, :] = x

        _norm_mod_blocks(h_ref, nrm_ref, sh_ref, sc_ref, emit)

    for src, dst in zip(side_in, side_out):
        dst[...] = src[...].astype(BF16)

    m = m_ref[...]
    wg, wu = wg_ref[...].astype(BF16), wu_ref[...].astype(BF16)
    if emit_w:
        w_out[0][...] = wg
        w_out[1][...] = wu
    g = jnp.dot(m, wg, preferred_element_type=F32)
    u = jnp.dot(m, wu, preferred_element_type=F32)
    a = (g * jax.nn.sigmoid(g) * u).astype(BF16)
    for nb in range(o_ref.shape[1] // FFN_OUT_BLK):
        cols = slice(nb * FFN_OUT_BLK, (nb + 1) * FFN_OUT_BLK)
        wd = wd_ref[:, cols].astype(BF16)
        if emit_w:
            w_out[2][:, cols] = wd
        p = jnp.dot(a, wd, preferred_element_type=F32)
        o_ref[:, cols] += (0.5 * gt_ref[:, cols]) * p


def _ffn(h, mods, layer, mod_base, norm_g, w_gate, w_up, w_down, *, tm, bidx, emit_w=False, cast_next=None,
         cast_mats=()):
    n_tok, d = h.shape
    f_dim = w_gate.shape[-1]
    if w_gate.ndim == 3:
        tf = 256
        wspec = lambda shape, idx: pl.BlockSpec((None,) + shape, lambda i, f: (layer,) + idx(f))
    else:
        assert not emit_w and w_gate.dtype == BF16
        tf = 512
        wspec = lambda shape, idx: pl.BlockSpec(shape, lambda i, f: idx(f))
    n_i, n_f = n_tok // tm, f_dim // tf
    out_specs = [pl.BlockSpec((tm, d), lambda i, f: (i, 0))]
    out_shape = [jax.ShapeDtypeStruct((n_tok, d), F32)]
    w_shapes = [jax.ShapeDtypeStruct((d, f_dim), BF16), jax.ShapeDtypeStruct((d, f_dim), BF16),
                jax.ShapeDtypeStruct((f_dim, d), BF16)]
    if emit_w:
        assert n_i == 1, "each weight tile must be visited exactly once"
        out_specs += [pl.BlockSpec((d, tf), lambda i, f: (0, f)), pl.BlockSpec((d, tf), lambda i, f: (0, f)),
                      pl.BlockSpec((tf, d), lambda i, f: (f, 0))]
        out_shape += w_shapes
    side_args, side_specs = [], []
    if cast_next is not None:
        assert not emit_w
        side_args, s_layer = list(cast_next[0]), cast_next[1]
        rb, cb = d // n_i, f_dim // n_f
        side_specs = [pl.BlockSpec((None, rb, cb), lambda i, f: (s_layer, i, f)),
                      pl.BlockSpec((None, rb, cb), lambda i, f: (s_layer, i, f)),
                      pl.BlockSpec((None, cb, rb), lambda i, f: (s_layer, f, i))]
        out_specs += [pl.BlockSpec((rb, cb), lambda i, f: (i, f)), pl.BlockSpec((rb, cb), lambda i, f: (i, f)),
                      pl.BlockSpec((cb, rb), lambda i, f: (f, i))]
        out_shape += w_shapes
    for arr, sub in cast_mats:
        assert not emit_w and n_f >= SIDE_COLS
        rb, cb = arr.shape[1] // n_i, arr.shape[2] // SIDE_COLS
        col = lambda f: jnp.minimum(f, SIDE_COLS - 1)
        side_args.append(arr)
        side_specs.append(pl.BlockSpec((None, rb, cb), lambda i, f, sub=sub: (sub, i, col(f))))
        out_specs.append(pl.BlockSpec((rb, cb), lambda i, f: (i, col(f))))
        out_shape.append(jax.ShapeDtypeStruct(arr.shape[1:], BF16))
    out = pl.pallas_call(
        functools.partial(_ffn_kernel, emit_w=emit_w, n_side=len(side_args)),
        grid=(n_i, n_f),
        in_specs=[
            pl.BlockSpec((tm, d), lambda i, f: (i, 0), pipeline_mode=pl.Buffered(1 if emit_w else 2)),
            pl.BlockSpec((None, 1, d), lambda i, f: (layer, 0, 0)),
            _mod_spec(layer, mod_base + 0, d, bidx),
            _mod_spec(layer, mod_base + 1, d, bidx),
            _mod_spec(layer, mod_base + 2, d, bidx),
            wspec((d, tf), lambda f: (0, f)),
            wspec((d, tf), lambda f: (0, f)),
            wspec((tf, d), lambda f: (f, 0)),
        ] + side_specs,
        out_specs=out_specs,
        out_shape=out_shape,
        scratch_shapes=[pltpu.VMEM((tm, d), BF16)],
        compiler_params=_cparams(("arbitrary", "arbitrary")),
        name="ffn",
    )(h, norm_g.reshape(norm_g.shape[0], 1, d), mods, mods, mods, w_gate, w_up, w_down, *side_args)
    return (out[0], tuple(out[1:])) if len(out) > 1 else out[0]


def _zoh(a_re, a_im, log_dt):
    dt = jnp.exp(log_dt)
    mag = jnp.exp(a_re * dt)
    l_re = mag * jnp.cos(a_im * dt)
    l_im = mag * jnp.sin(a_im * dt)
    den = a_re * a_re + a_im * a_im
    c_re = ((l_re - 1.0) * a_re + l_im * a_im) / den
    c_im = (l_im * a_re - (l_re - 1.0) * a_im) / den
    return l_re, l_im, c_re, c_im


def _lane_tile(x, reps):
    w = x.shape[1]
    row = lax.broadcasted_iota(jnp.int32, (w, w * reps), 0)
    col = lax.broadcasted_iota(jnp.int32, (w, w * reps), 1)
    sel = (row == col % w).astype(BF16)
    return jnp.dot(x, sel, preferred_element_type=F32)


def _s5_param_kernel(are_ref, aim_ref, ldt_ref, bre_ref, bim_ref, cre_ref, cim_ref, ar2_ref, ai2_ref, ld2_ref,
                     obr_ref, obi_ref, ocr_ref, oci_ref, olr_ref, oli_ref):
    gpb, _, n_state = are_ref.shape
    _, _, k_re, k_im = _zoh(are_ref[...], aim_ref[...], ldt_ref[...])
    rows = (gpb * S5_GROUP, n_state)
    k_re = jnp.broadcast_to(k_re, (gpb, S5_GROUP, n_state)).reshape(rows)
    k_im = jnp.broadcast_to(k_im, (gpb, S5_GROUP, n_state)).reshape(rows)
    bb_re = k_re * bre_ref[...] - k_im * bim_ref[...]
    bb_im = k_re * bim_ref[...] + k_im * bre_ref[...]
    n_in, n_st = bb_re.shape[0], S5_STATE * gpb
    on_diag = (lax.broadcasted_iota(jnp.int32, (n_in, n_st), 0) // S5_GROUP
               == lax.broadcasted_iota(jnp.int32, (n_in, n_st), 1) // S5_STATE)
    obr_ref[...] = jnp.where(on_diag, _lane_tile(bb_re.astype(BF16), gpb), 0.0).astype(BF16)
    obi_ref[...] = jnp.where(on_diag, _lane_tile(bb_im.astype(BF16), gpb), 0.0).astype(BF16)
    on_diag_t = (lax.broadcasted_iota(jnp.int32, (n_st, n_in), 0) // S5_STATE
                 == lax.broadcasted_iota(jnp.int32, (n_st, n_in), 1) // S5_GROUP)
    ocr_ref[...] = jnp.where(on_diag_t, _lane_tile(cre_ref[...].astype(BF16), gpb), 0.0).astype(BF16)
    oci_ref[...] = jnp.where(on_diag_t, _lane_tile((-cim_ref[...]).astype(BF16), gpb), 0.0).astype(BF16)
    l_re, l_im, _, _ = _zoh(ar2_ref[...], ai2_ref[...], ld2_ref[...])
    olr_ref[...] = l_re
    oli_ref[...] = l_im


def _s5_params(a_re, a_im, log_dt, b_re, b_im, c_re, c_im):
    dirs, g, p = a_re.shape
    hch = b_re.shape[3]
    gpb = g // S5_GB
    rows_in, rows_st = gpb * hch, gpb * p

    per_group = lambda v: v.reshape(dirs, S5_GB, gpb, 1, p)
    ldt = jnp.broadcast_to(log_dt[:, :, None], (dirs, g, p))
    args = (
        per_group(a_re), per_group(a_im), per_group(ldt),
        jnp.swapaxes(b_re, 2, 3).reshape(dirs, S5_GB, rows_in, p),
        jnp.swapaxes(b_im, 2, 3).reshape(dirs, S5_GB, rows_in, p),
        jnp.swapaxes(c_re, 2, 3).reshape(dirs, S5_GB, rows_st, hch),
        jnp.swapaxes(c_im, 2, 3).reshape(dirs, S5_GB, rows_st, hch),
        a_re.reshape(dirs, g * p // V7X_LANES, V7X_LANES),
        a_im.reshape(dirs, g * p // V7X_LANES, V7X_LANES),
        ldt.reshape(dirs, g * p // V7X_LANES, V7X_LANES),
    )
    blk = lambda r, c: pl.BlockSpec((None, None, r, c), lambda dd, gb: (dd, gb, 0, 0))
    flat = pl.BlockSpec((None, g * p // V7X_LANES, V7X_LANES), lambda dd, gb: (dd, 0, 0))
    return pl.pallas_call(
        _s5_param_kernel,
        grid=(dirs, S5_GB),
        in_specs=[pl.BlockSpec((None, None, gpb, 1, p), lambda dd, gb: (dd, gb, 0, 0, 0))] * 3
        + [blk(rows_in, p)] * 2 + [blk(rows_st, hch)] * 2 + [flat] * 3,
        out_specs=[blk(rows_in, rows_st), blk(rows_in, rows_st), blk(rows_st, rows_in), blk(rows_st, rows_in),
                   flat, flat],
        out_shape=[
            jax.ShapeDtypeStruct((dirs, S5_GB, rows_in, rows_st), BF16),
            jax.ShapeDtypeStruct((dirs, S5_GB, rows_in, rows_st), BF16),
            jax.ShapeDtypeStruct((dirs, S5_GB, rows_st, rows_in), BF16),
            jax.ShapeDtypeStruct((dirs, S5_GB, rows_st, rows_in), BF16),
            jax.ShapeDtypeStruct((dirs, g * p // V7X_LANES, V7X_LANES), F32),
            jax.ShapeDtypeStruct((dirs, g * p // V7X_LANES, V7X_LANES), F32),
        ],
        compiler_params=_cparams(("arbitrary", "arbitrary")),
        name="s5_params",
    )(*args)


def _pair_rows(lam):
    dirs, tiles, lanes = lam.shape
    half = V7X_SUBLANES // 2
    x = jnp.broadcast_to(lam.reshape(dirs, 2, tiles // 2, 1, lanes), (dirs, 2, tiles // 2, half, lanes))
    return jnp.transpose(x, (0, 2, 1, 3, 4)).reshape(dirs, tiles // 2, V7X_SUBLANES, lanes)


def _s5_scan_kernel(*refs, n_ctx_chunks, tq, pitch, reverse, ctx_row, emit_z):
    gb_per_half = S5_GB // 2
    n_in = 16 if emit_z else 13
    (hl_ref, hc_ref, nrm_ref, sh_ref, sc_ref, bre_ref, bim_ref, cre_ref, cim_ref, are_ref, aim_ref) = refs[:11]
    if emit_z:
        yl_ref, yc_ref, dsk_ref = refs[11:14]
    ol_ref, oc_ref = refs[n_in - 2:n_in]
    u_st = refs[n_in]
    s_re_k = refs[n_in + 1:n_in + 1 + gb_per_half]
    s_im_k = refs[n_in + 1 + gb_per_half:n_in + 1 + 2 * gb_per_half]
    h_re, h_im = refs[n_in + 1 + 2 * gb_per_half:]
    c = pl.program_id(0)
    n_b = hl_ref.shape[0]
    half_rows = n_b * pitch
    tiles_per_gb = s_re_k[0].shape[0]
    lanes_in = u_st.shape[1] // S5_GB

    @pl.when(c == 0)
    def _():
        h_re[...] = jnp.zeros_like(h_re)
        h_im[...] = jnp.zeros_like(h_im)
        u_st[...] = jnp.zeros_like(u_st)

    is_ctx = c < n_ctx_chunks

    @pl.when(is_ctx)
    def _():
        for b in range(n_b):
            u_st[b * pitch:b * pitch + tq, :] = _norm_mod(hc_ref[b], nrm_ref[...], sh_ref[ctx_row], sc_ref[ctx_row])

    @pl.when(jnp.logical_not(is_ctx))
    def _():
        for b in range(n_b):
            u_st[b * pitch:b * pitch + tq, :] = _norm_mod(hl_ref[b], nrm_ref[...], sh_ref[b], sc_ref[b])

    def project_in(gb):
        lhs = u_st[:, gb * lanes_in:(gb + 1) * lanes_in].astype(BF16)
        p_re = jnp.dot(lhs, bre_ref[gb], preferred_element_type=F32)
        p_im = jnp.dot(lhs, bim_ref[gb], preferred_element_type=F32)
        half = gb // gb_per_half
        rows = slice(half * half_rows, (half + 1) * half_rows)
        s_re, s_im = s_re_k[gb % gb_per_half], s_im_k[gb % gb_per_half]
        for j in range(tiles_per_gb):
            s_re[j, rows, :] = p_re[:, j * V7X_LANES:(j + 1) * V7X_LANES]
            s_im[j, rows, :] = p_im[:, j * V7X_LANES:(j + 1) * V7X_LANES]

    def scan_pass(k):
        s_re, s_im = s_re_k[k], s_im_k[k]
        ns = list(range(k * tiles_per_gb, (k + 1) * tiles_per_gb))
        a_r = [are_ref[n] for n in ns]
        a_i = [aim_ref[n] for n in ns]
        hr = [h_re[n] for n in ns]
        hi = [h_im[n] for n in ns]
        for t in range(tq):
            rows = pl.ds(tq - 1 - t if reverse else t, V7X_SUBLANES, stride=pitch)
            for j in range(tiles_per_gb):
                n_r = a_r[j] * hr[j] - a_i[j] * hi[j] + s_re[j, rows, :]
                n_i = a_r[j] * hi[j] + a_i[j] * hr[j] + s_im[j, rows, :]
                s_re[j, rows, :] = n_r
                s_im[j, rows, :] = n_i
                hr[j], hi[j] = n_r, n_i
        for j, n in enumerate(ns):
            h_re[n] = hr[j]
            h_im[n] = hi[j]

    def project_out(gb):
        half = gb // gb_per_half
        rows = slice(half * half_rows, (half + 1) * half_rows)
        s_re, s_im = s_re_k[gb % gb_per_half], s_im_k[gb % gb_per_half]
        l_re = jnp.concatenate([s_re[j, rows, :] for j in range(tiles_per_gb)], axis=1).astype(BF16)
        l_im = jnp.concatenate([s_im[j, rows, :] for j in range(tiles_per_gb)], axis=1).astype(BF16)
        return (jnp.dot(l_re, cre_ref[gb], preferred_element_type=F32)
                + jnp.dot(l_im, cim_ref[gb], preferred_element_type=F32))

    for gb in range(S5_GB):
        project_in(gb)
    for k in range(gb_per_half):
        scan_pass(k)
    ys = [project_out(gb) for gb in range(S5_GB)]

    def emit(out_ref, yf_ref):
        for gb in range(S5_GB):
            cols = slice(gb * lanes_in, (gb + 1) * lanes_in)
            for b in range(n_b):
                y = ys[gb][b * pitch:b * pitch + tq]
                if emit_z:
                    y = y + yf_ref[b, :, cols] + dsk_ref[:, cols] * u_st[b * pitch:b * pitch + tq, cols]
                    out_ref[b, :, cols] = _gelu_tanh(y).astype(out_ref.dtype)
                else:
                    out_ref[b, :, cols] = y

    @pl.when(is_ctx)
    def _():
        emit(oc_ref, yc_ref if emit_z else None)

    @pl.when(jnp.logical_not(is_ctx))
    def _():
        emit(ol_ref, yl_ref if emit_z else None)


def _gelu_tanh(x):
    return 0.5 * x * (1.0 + jnp.tanh(math.sqrt(2.0 / math.pi) * (x + 0.044715 * (x * x * x))))


def _s5_scan(h_lat, h_ctx, mods, layer, mod_base, norm_g, maps, direction, y_prev=None, d_skip=None):
    n_b, s_len, d = h_lat.shape
    l_len = h_ctx.shape[1]
    tq, pitch = S5_TQ, S5_PITCH
    n_l, n_s = l_len // tq, s_len // tq
    bd_bre, bd_bim, bd_cre, bd_cim, a_re, a_im = maps
    n_pairs = a_re.shape[1]
    reverse = direction == 1
    emit_z = y_prev is not None

    def lat_idx(c):
        k = jnp.maximum(c - n_l, 0)
        return n_s - 1 - k if reverse else k

    def ctx_idx(c):
        k = jnp.minimum(c, n_l - 1)
        return n_l - 1 - k if reverse else k

    def wspec(arr):
        shape = arr.shape[1:]
        return _resident((None,) + shape, lambda c: (direction,) + (0,) * len(shape))

    lat_spec = pl.BlockSpec((n_b, tq, d), lambda c: (0, lat_idx(c), 0))
    ctx_spec = pl.BlockSpec((n_b, tq, d), lambda c: (0, ctx_idx(c), 0))
    mod_rows = lambda col: pl.BlockSpec((None, COND_ROWS, 1, d), lambda c: (layer, 0, 0, col))
    in_specs = [lat_spec, ctx_spec,
                pl.BlockSpec((None, 1, d), lambda c: (layer, 0, 0)),
                mod_rows(mod_base), mod_rows(mod_base + 1),
                wspec(bd_bre), wspec(bd_bim), wspec(bd_cre), wspec(bd_cim), wspec(a_re), wspec(a_im)]
    args = [h_lat, h_ctx, norm_g.reshape(norm_g.shape[0], 1, d), mods, mods,
            bd_bre, bd_bim, bd_cre, bd_cim, a_re, a_im]
    out_dtype = F32
    if emit_z:
        in_specs += [lat_spec, ctx_spec, pl.BlockSpec((None, 1, d), lambda c: (0, 0, 0))]
        args += [y_prev[0], y_prev[1], d_skip.reshape(d_skip.shape[0], 1, d)]
        out_dtype = BF16
    kern = functools.partial(_s5_scan_kernel, n_ctx_chunks=n_l, tq=tq, pitch=pitch, reverse=reverse,
                             ctx_row=n_b, emit_z=emit_z)
    return pl.pallas_call(
        kern,
        grid=(n_l + n_s,),
        in_specs=in_specs,
        out_specs=[lat_spec, ctx_spec],
        out_shape=[jax.ShapeDtypeStruct((n_b, s_len, d), out_dtype),
                   jax.ShapeDtypeStruct((n_b, l_len, d), out_dtype)],
        scratch_shapes=[
            pltpu.VMEM((n_b * pitch, d), F32),
            *[pltpu.VMEM((2 * n_pairs // S5_GB, 2 * n_b * pitch, V7X_LANES), F32) for _ in range(S5_GB)],
            pltpu.VMEM((n_pairs, V7X_SUBLANES, V7X_LANES), F32),
            pltpu.VMEM((n_pairs, V7X_SUBLANES, V7X_LANES), F32),
        ],
        compiler_params=_cparams(("arbitrary",)),
        name="s5_bwd" if reverse else "s5_fwd",
    )(*args)


def _glu_kernel(z_ref, w1_ref, w2_ref, h_ref, gt_ref, o_ref):
    z = z_ref[...]
    for nb in range(o_ref.shape[1] // COL_BLK):
        cols = slice(nb * COL_BLK, (nb + 1) * COL_BLK)
        a = jnp.dot(z, w1_ref[:, cols], preferred_element_type=F32)
        b = jnp.dot(z, w2_ref[:, cols], preferred_element_type=F32)
        o_ref[:, cols] = h_ref[:, cols] + gt_ref[:, cols] * (a * jax.nn.sigmoid(b))


def _glu(z, w1, w2, sub, h, mods, layer, gate_col, *, tm, bidx):
    n_tok, d = h.shape
    return pl.pallas_call(
        _glu_kernel,
        grid=(n_tok // tm,),
        in_specs=[
            pl.BlockSpec((tm, d), lambda i: (i, 0)),
            _resident((None, d, d), lambda i: (sub, 0, 0)),
            _resident((None, d, d), lambda i: (sub, 0, 0)),
            pl.BlockSpec((tm, d), lambda i: (i, 0)),
            _mod_spec(layer, gate_col, d, bidx),
        ],
        out_specs=pl.BlockSpec((tm, d), lambda i: (i, 0)),
        out_shape=jax.ShapeDtypeStruct((n_tok, d), F32),
        compiler_params=_cparams(("arbitrary",)),
        name="s5_glu",
    )(z, w1, w2, h, mods)


def _rope_tables(seq):
    pairs = HEAD_DIM // 4
    freqs = ROPE_THETA ** (-jnp.arange(pairs, dtype=F32) / pairs)
    pos = jnp.arange(seq, dtype=jnp.int32)
    ang_r = (pos // GRID_W).astype(F32)[:, None] * freqs
    ang_c = (pos % GRID_W).astype(F32)[:, None] * freqs
    cos = jnp.concatenate([jnp.cos(ang_r), jnp.cos(ang_r), jnp.cos(ang_c), jnp.cos(ang_c)], axis=-1)
    sin = jnp.concatenate([-jnp.sin(ang_r), jnp.sin(ang_r), -jnp.sin(ang_c), jnp.sin(ang_c)], axis=-1)
    return cos, sin


def _qkv_kernel(h_ref, nrm_ref, sh_ref, sc_ref, w_ref, qg_ref, kg_ref, cos_ref, sin_ref, o_ref,
                *, head0, n_q_heads, n_kv_heads, rope):
    m = _norm_mod(h_ref[...], nrm_ref[...], sh_ref[...], sc_ref[...]).astype(BF16)
    lane = lax.broadcasted_iota(jnp.int32, (1, HEAD_DIM), 1)
    first = (lane % (HEAD_DIM // 2)) < (HEAD_DIM // 4)
    hpb = COL_BLK // HEAD_DIM
    for nb in range(w_ref.shape[1] // COL_BLK):
        acc = jnp.dot(m, w_ref[:, nb * COL_BLK:(nb + 1) * COL_BLK], preferred_element_type=F32)
        for hh in range(hpb):
            head = head0 + nb * hpb + hh
            x = acc[:, hh * HEAD_DIM:(hh + 1) * HEAD_DIM]
            if head < n_q_heads + n_kv_heads:
                gain = qg_ref[...] if head < n_q_heads else kg_ref[...]
                xn = x * lax.rsqrt(jnp.mean(x * x, axis=-1, keepdims=True) + EPS) * gain
                if rope:
                    partner = jnp.where(first, pltpu.roll(xn, HEAD_DIM - HEAD_DIM // 4, 1),
                                        pltpu.roll(xn, HEAD_DIM // 4, 1))
                    xn = xn * cos_ref[...] + partner * sin_ref[...]
                x = xn
            o_ref[nb * hpb + hh] = x.astype(BF16)


def _qkv(h, mods, layer, mod_base, norm_g, w_qkv, sub, q_gain, k_gain, cos, sin, *, seq, tm, bidx,
         n_q_heads, n_kv_heads, kv_only, rope):
    n_tok, d = h.shape
    n_b = n_tok // seq
    n_cols = w_qkv.shape[2]
    head0 = n_q_heads if kv_only else 0
    width = n_cols - head0 * HEAD_DIM
    heads = width // HEAD_DIM
    assert (head0 * HEAD_DIM) % width == 0
    tiles_per_seq = seq // tm
    kern = functools.partial(_qkv_kernel, head0=head0, n_q_heads=n_q_heads, n_kv_heads=n_kv_heads, rope=rope)
    return pl.pallas_call(
        kern,
        grid=(n_tok // tm,),
        in_specs=[
            pl.BlockSpec((tm, d), lambda i: (i, 0)),
            pl.BlockSpec((None, 1, d), lambda i: (layer, 0, 0)),
            _mod_spec(layer, mod_base + 0, d, bidx),
            _mod_spec(layer, mod_base + 1, d, bidx),
            _resident((None, d, width), lambda i: (sub, 0, head0 * HEAD_DIM // width)),
            pl.BlockSpec((None, 1, HEAD_DIM), lambda i: (sub, 0, 0)),
            pl.BlockSpec((None, 1, HEAD_DIM), lambda i: (sub, 0, 0)),
            pl.BlockSpec((tm, HEAD_DIM), lambda i: (i % tiles_per_seq, 0)),
            pl.BlockSpec((tm, HEAD_DIM), lambda i: (i % tiles_per_seq, 0)),
        ],
        out_specs=pl.BlockSpec((None, heads, tm, HEAD_DIM), lambda i: (i // tiles_per_seq, 0, i % tiles_per_seq, 0)),
        out_shape=jax.ShapeDtypeStruct((n_b, heads, seq, HEAD_DIM), BF16),
        compiler_params=_cparams(("arbitrary",)),
        name="qkv",
    )(h, norm_g.reshape(norm_g.shape[0], 1, d), mods, mods, w_qkv,
      q_gain.reshape(q_gain.shape[0], 1, HEAD_DIM), k_gain.reshape(k_gain.shape[0], 1, HEAD_DIM), cos, sin)


def _attn_kernel(q_ref, kl_ref, vl_ref, kc_ref, vc_ref, o_ref):
    rep, tq, hd = q_ref.shape
    dn = (((1,), (1,)), ((), ()))
    c = (HEAD_DIM ** -0.5) * math.log2(math.e)
    blk = min(ATTN_ROWS, tq)
    chains = [(r, slice(sb * blk, (sb + 1) * blk)) for r in range(rep) for sb in range(tq // blk)]

    def scores(chain):
        r, rows = chain
        q = q_ref[r, rows, :]
        return (lax.dot_general(q, kl_ref[...], dn, preferred_element_type=F32),
                lax.dot_general(q, kc_ref[...], dn, preferred_element_type=F32))

    ahead = [scores(ch) for ch in chains[:ATTN_AHEAD]]
    for i, (r, rows) in enumerate(chains):
        s_l, s_c = ahead.pop(0)
        if i + ATTN_AHEAD < len(chains):
            ahead.append(scores(chains[i + ATTN_AHEAD]))
        m = jnp.maximum(jnp.max(s_l, axis=-1, keepdims=True), jnp.max(s_c, axis=-1, keepdims=True))
        p_l = jnp.exp2((s_l - m) * c)
        p_c = jnp.exp2((s_c - m) * c)
        den = jnp.sum(p_l, axis=-1, keepdims=True) + jnp.sum(p_c, axis=-1, keepdims=True)
        o = (jnp.dot(p_l.astype(BF16), vl_ref[...], preferred_element_type=F32)
             + jnp.dot(p_c.astype(BF16), vc_ref[...], preferred_element_type=F32)) / den
        o_ref[rows, r * hd:(r + 1) * hd] = o.astype(BF16)


def _attention(qkv_lat, kv_ctx, n_q_heads, n_kv_heads, *, tq=256):
    n_b, _, seq, hd = qkv_lat.shape
    l_len = kv_ctx.shape[2]
    rep = n_q_heads // n_kv_heads
    return pl.pallas_call(
        _attn_kernel,
        grid=(n_b, n_kv_heads, seq // tq),
        in_specs=[
            pl.BlockSpec((None, rep, tq, hd), lambda b, g, i: (b, g, i, 0)),
            pl.BlockSpec((None, None, seq, hd), lambda b, g, i: (b, n_q_heads + g, 0, 0)),
            pl.BlockSpec((None, None, seq, hd), lambda b, g, i: (b, n_q_heads + n_kv_heads + g, 0, 0)),
            pl.BlockSpec((None, None, l_len, hd), lambda b, g, i: (b, g, 0, 0)),
            pl.BlockSpec((None, None, l_len, hd), lambda b, g, i: (b, n_kv_heads + g, 0, 0)),
        ],
        out_specs=pl.BlockSpec((None, tq, rep * hd), lambda b, g, i: (b, i, g)),
        out_shape=jax.ShapeDtypeStruct((n_b, seq, n_q_heads * hd), BF16),
        compiler_params=_cparams(("arbitrary", "arbitrary", "arbitrary")),
        name="attention",
    )(qkv_lat, qkv_lat, qkv_lat, kv_ctx, kv_ctx)


def _oproj_kernel(x_ref, w_ref, h_ref, gt_ref, o_ref):
    x = x_ref[...]
    for nb in range(o_ref.shape[1] // COL_BLK):
        cols = slice(nb * COL_BLK, (nb + 1) * COL_BLK)
        acc = jnp.dot(x, w_ref[:, cols], preferred_element_type=F32)
        o_ref[:, cols] = h_ref[:, cols] + gt_ref[:, cols] * acc


def _oproj(x, w_o, sub, h, mods, layer, gate_col, *, tm, bidx):
    n_tok, d = h.shape
    k_dim = x.shape[1]
    return pl.pallas_call(
        _oproj_kernel,
        grid=(n_tok // tm,),
        in_specs=[
            pl.BlockSpec((tm, k_dim), lambda i: (i, 0)),
            _resident((None, k_dim, d), lambda i: (sub, 0, 0)),
            pl.BlockSpec((tm, d), lambda i: (i, 0)),
            _mod_spec(layer, gate_col, d, bidx),
        ],
        out_specs=pl.BlockSpec((tm, d), lambda i: (i, 0)),
        out_shape=jax.ShapeDtypeStruct((n_tok, d), F32),
        compiler_params=_cparams(("arbitrary",)),
        name="oproj",
    )(x, w_o, h, mods)


def kernel(x, c, ctx, c_ctx, ada_w, ada_b, norm_ffn1, norm_mix, norm_ffn2, ffn1_w_gate, ffn1_w_up, ffn1_w_down, ffn2_w_gate, ffn2_w_up, ffn2_w_down, s5_a_re, s5_a_im, s5_log_dt, s5_b_re, s5_b_im, s5_c_re, s5_c_im, s5_d, s5_glu_w1, s5_glu_w2, attn_w_qkv, attn_w_o, attn_q_gain, attn_k_gain):
    n_b, seq, d = x.shape
    l_len = ctx.shape[1]
    depth = ada_w.shape[0]
    assert depth == 2 and ada_w.shape[2] == N_MOD * d
    assert n_b == 4, "the S5 scan packs 4 sequences x 2 lane tiles onto the 8 sublanes"
    n_q_heads = d // HEAD_DIM
    n_kv_heads = n_q_heads // KV_REP

    tm_lat = min(1024, seq)
    tm_ctx = min(1024, n_b * l_len)
    tm_mm = min(512, seq)
    tm_qkv = min(256, seq)
    tm_mm_ctx = min(512, l_len)
    ctx_row = n_b
    bidx_lat = lambda tm: (lambda i: (i * tm) // seq)
    bidx_ctx = lambda i: ctx_row

    cond = jnp.concatenate([c, c_ctx[None, :], jnp.zeros((COND_ROWS - n_b - 1, d), F32)], axis=0)
    mods = _ada_mods(cond, ada_w, ada_b).reshape(depth, COND_ROWS, 1, N_MOD * d)

    ffn1 = (ffn1_w_gate, ffn1_w_up, ffn1_w_down)
    ffn2 = (ffn2_w_gate, ffn2_w_up, ffn2_w_down)

    h_lat = x.reshape(n_b * seq, d)
    h_ctx = ctx.reshape(n_b * l_len, d)

    def ffn_pair(h_lat, h_ctx, layer, mod_base, norm_g, weights, cast_next=None, cast_mats=()):
        h_ctx, w_bf16 = _ffn(h_ctx, mods, layer, mod_base, norm_g, *weights, tm=tm_ctx, bidx=bidx_ctx, emit_w=True)
        h_lat, cast = _ffn(h_lat, mods, layer, mod_base, norm_g, *w_bf16, tm=tm_lat, bidx=bidx_lat(tm_lat),
                           cast_next=cast_next, cast_mats=cast_mats)
        return h_lat, h_ctx, cast

    layer = 0
    h_lat, h_ctx, (glu_w1, glu_w2) = ffn_pair(h_lat, h_ctx, layer, 0, norm_ffn1, ffn1,
                                              cast_mats=((s5_glu_w1, 0), (s5_glu_w2, 0)))
    glu_w1, glu_w2 = glu_w1[None], glu_w2[None]

    params = _s5_params(s5_a_re[0], s5_a_im[0], s5_log_dt[0], s5_b_re[0], s5_b_im[0], s5_c_re[0], s5_c_im[0])
    maps = tuple(params[:4]) + (_pair_rows(params[4]), _pair_rows(params[5]))
    h3_lat, h3_ctx = h_lat.reshape(n_b, seq, d), h_ctx.reshape(n_b, l_len, d)
    y_fwd = _s5_scan(h3_lat, h3_ctx, mods, layer, 3, norm_mix, maps, 0)
    z_lat, z_ctx = _s5_scan(h3_lat, h3_ctx, mods, layer, 3, norm_mix, maps, 1, y_prev=y_fwd, d_skip=s5_d)
    h_lat = _glu(z_lat.reshape(n_b * seq, d), glu_w1, glu_w2, 0, h_lat, mods, layer, 5,
                 tm=tm_mm, bidx=bidx_lat(tm_mm))
    h_ctx = _glu(z_ctx.reshape(n_b * l_len, d), glu_w1, glu_w2, 0, h_ctx, mods, layer, 5,
                 tm=tm_mm_ctx, bidx=bidx_ctx)

    h_lat, h_ctx, (w_qkv, w_o) = ffn_pair(h_lat, h_ctx, layer, 6, norm_ffn2, ffn2,
                                          cast_mats=((attn_w_qkv, 0), (attn_w_o, 0)))
    w_qkv, w_o = w_qkv[None], w_o[None]

    layer = 1
    h_lat, h_ctx, ffn2_last = ffn_pair(h_lat, h_ctx, layer, 0, norm_ffn1, ffn1, cast_next=(ffn2, layer))

    cos, sin = _rope_tables(seq)
    qkv_lat = _qkv(h_lat, mods, layer, 3, norm_mix, w_qkv, 0, attn_q_gain, attn_k_gain, cos, sin,
                   seq=seq, tm=tm_qkv, bidx=bidx_lat(tm_qkv), n_q_heads=n_q_heads, n_kv_heads=n_kv_heads,
                   kv_only=False, rope=True)
    no_rope = jnp.zeros((l_len, HEAD_DIM), F32)
    kv_ctx = _qkv(h_ctx, mods, layer, 3, norm_mix, w_qkv, 0, attn_q_gain, attn_k_gain, no_rope, no_rope,
                  seq=l_len, tm=tm_mm_ctx, bidx=bidx_ctx, n_q_heads=n_q_heads, n_kv_heads=n_kv_heads,
                  kv_only=True, rope=False)
    o_lat = _attention(qkv_lat, kv_ctx, n_q_heads, n_kv_heads, tq=min(1024, seq))
    h_lat = _oproj(o_lat.reshape(n_b * seq, d), w_o, 0, h_lat, mods, layer, 5,
                   tm=tm_mm, bidx=bidx_lat(tm_mm))

    h_lat = _ffn(h_lat, mods, layer, 6, norm_ffn2, *ffn2_last, tm=tm_lat, bidx=bidx_lat(tm_lat))
    return h_lat.reshape(n_b, seq, d)
```
